```python
import math
import jax, jax.numpy as jnp
from jax import lax
import numpy as np

D_MODEL = 1024
BATCH = 16
SEQ = 2048
DEPTH = 1
DEC_BATCH = 128
DEC_SEQ = 8
PAST_LEN = 8192
PAGE_SIZE = 128

HEAD_DIM = 64
N_HEADS_A = D_MODEL // 128
N_KV_A = 2
N_HEADS_B = D_MODEL // 128
N_KV_B = 2
WIDTH_A = N_HEADS_A * HEAD_DIM
WIDTH_B = N_HEADS_B * HEAD_DIM
IDX_HEADS = 8
IDX_DIM = 64
IDX_TOPK = 256
MOBA_BLOCK = 256
MOBA_TOPK = 3
ROPE_THETA = 500000.0
ROT_FRACTION = 4
RMS_EPS = 1e-6
DSA_ROWS = 1024
MOBA_ROWS = 128
SPLITS = (WIDTH_A, N_KV_A * HEAD_DIM, N_KV_A * HEAD_DIM,
          IDX_HEADS * IDX_DIM, IDX_DIM, IDX_HEADS,
          WIDTH_A,
          WIDTH_B, N_KV_B * HEAD_DIM, N_KV_B * HEAD_DIM,
          WIDTH_B,
          D_MODEL, D_MODEL)
IN_WIDTH = sum(SPLITS)

kernel_name = "dsa_moba_gated_hybrid_step"


def rms_norm(x, g):
    xf = x.astype(jnp.float32)
    y = xf * lax.rsqrt(jnp.mean(xf * xf, axis=-1, keepdims=True) + RMS_EPS)
    return (y * g.astype(jnp.float32)).astype(x.dtype)


def rope_partial(x, pos):
    d = x.shape[-1]
    rot = d // ROT_FRACTION
    half = rot // 2
    inv = ROPE_THETA ** (-jnp.arange(half, dtype=jnp.float32) / half)
    ang = pos.astype(jnp.float32)[:, None] * inv[None, :]
    cos = jnp.cos(ang)[None, :, None, :]
    sin = jnp.sin(ang)[None, :, None, :]
    xf = x.astype(jnp.float32)
    x1 = xf[..., :half]
    x2 = xf[..., half:rot]
    out = jnp.concatenate([x1 * cos - x2 * sin, x2 * cos + x1 * sin, xf[..., rot:]], axis=-1)
    return out.astype(x.dtype)


def map_query_chunks(fn, qpos, xs, chunk):
    B, T = xs[0].shape[:2]
    n = T // chunk
    xs_c = [jnp.moveaxis(a.reshape((B, n, chunk) + a.shape[2:]), 1, 0) for a in xs]
    out = lax.map(fn, [qpos.reshape(n, chunk)] + xs_c)
    return jnp.moveaxis(out, 0, 1).reshape((B, T) + out.shape[3:])


def gather_past(pool, page_table):
    rows = pool[page_table]
    B, n_pages = page_table.shape
    return rows.reshape((B, n_pages * PAGE_SIZE) + pool.shape[2:])


def fetch_paged_rows(pool, new_rows, page_table, pos):
    past_len = page_table.shape[1] * PAGE_SIZE
    bidx = jnp.arange(pos.shape[0]).reshape((-1,) + (1,) * (pos.ndim - 1))
    pc = jnp.clip(pos, 0, past_len - 1)
    phys = page_table[bidx, pc // PAGE_SIZE]
    from_pool = pool[phys, pc % PAGE_SIZE]
    from_new = new_rows[bidx, jnp.clip(pos - past_len, 0, new_rows.shape[1] - 1)]
    in_past = (pos < past_len).reshape(pos.shape + (1,) * (pool.ndim - 2))
    return jnp.where(in_past, from_pool, from_new)


def project(x, pos, norm_g, w_in, qn_a, kn_a, qn_i, kn_i, qn_b, kn_b):
    B, T, _ = x.shape
    h = rms_norm(x, norm_g)
    z = jnp.einsum('btd,de->bte', h, w_in)
    points = np.cumsum(SPLITS)[:-1].tolist()
    qa, ka, va, qi, ki, wi, za, qb, kb, vb, zb, ga, gb = jnp.split(z, points, axis=-1)
    qa = rope_partial(rms_norm(qa.reshape(B, T, N_HEADS_A, HEAD_DIM), qn_a), pos)
    ka = rope_partial(rms_norm(ka.reshape(B, T, N_KV_A, HEAD_DIM), kn_a), pos)
    kv_a = jnp.stack([ka, va.reshape(B, T, N_KV_A, HEAD_DIM)], axis=3)
    qi = rope_partial(rms_norm(qi.reshape(B, T, IDX_HEADS, IDX_DIM), qn_i), pos)
    ki = rope_partial(rms_norm(ki.reshape(B, T, 1, IDX_DIM), kn_i), pos)[:, :, 0]
    wi = wi * (IDX_HEADS * IDX_DIM) ** -0.5
    qb = rope_partial(rms_norm(qb.reshape(B, T, N_HEADS_B, HEAD_DIM), qn_b), pos)
    kb = rope_partial(rms_norm(kb.reshape(B, T, N_KV_B, HEAD_DIM), kn_b), pos)
    kv_b = jnp.stack([kb, vb.reshape(B, T, N_KV_B, HEAD_DIM)], axis=3)
    return qa, kv_a, qi, ki, wi, za, qb, kv_b, zb, ga, gb


def dsa_attention(q, qi, wi, qpos, ki_all, fetch_kv):
    B, T = q.shape[:2]
    L = ki_all.shape[1]
    k_sel = min(IDX_TOPK, L // 4)
    chunk = math.gcd(T, max(1, DSA_ROWS // B))
    kpos = jnp.arange(L, dtype=jnp.int32)
    kif = ki_all.astype(jnp.float32)
    group = N_HEADS_A // N_KV_A

    def one_chunk(args):
        pc, qc, qic, wic = args
        C = pc.shape[0]
        s = jax.nn.relu(jnp.einsum('bchd,bsd->bchs', qic.astype(jnp.float32), kif))
        score = jnp.einsum('bchs,bch->bcs', s, wic.astype(jnp.float32))
        score = jnp.where(kpos[None, None, :] <= pc[None, :, None], score, -jnp.inf)
        top_val, top_idx = lax.top_k(score, k_sel)
        keep = top_val > -jnp.inf
        kv = fetch_kv(top_idx)
        qg = qc.astype(jnp.float32).reshape(B, C, N_KV_A, group, HEAD_DIM)
        logits = jnp.einsum('bcgrd,bckgd->bcgrk', qg, kv[..., 0, :].astype(jnp.float32)) * HEAD_DIM ** -0.5
        logits = jnp.where(keep[:, :, None, None, :], logits, -jnp.inf)
        p = jax.nn.softmax(logits, axis=-1)
        o = jnp.einsum('bcgrk,bckgd->bcgrd', p, kv[..., 1, :].astype(jnp.float32))
        return o.reshape(B, C, N_HEADS_A, HEAD_DIM).astype(qc.dtype)

    return map_query_chunks(one_chunk, qpos, [q, qi, wi], chunk)


def moba_attention(q, qpos, kv_all):
    B, T = q.shape[:2]
    L = kv_all.shape[1]
    nb = max(-(-L // MOBA_BLOCK), MOBA_TOPK)
    kv_pad = jnp.pad(kv_all, ((0, 0), (0, nb * MOBA_BLOCK - L), (0, 0), (0, 0), (0, 0)))
    blocks = kv_pad.reshape(B, nb, MOBA_BLOCK, N_KV_B, 2, HEAD_DIM)
    k_mean = jnp.mean(blocks[:, :, :, :, 0, :].astype(jnp.float32), axis=2)
    head_kv = jnp.arange(N_HEADS_B) // (N_HEADS_B // N_KV_B)
    k_mean_h = k_mean[:, :, head_kv]
    chunk = math.gcd(T, max(1, MOBA_ROWS // B))
    offs = jnp.arange(MOBA_BLOCK, dtype=jnp.int32)
    bidx = jnp.arange(B)[:, None, None, None]

    def one_chunk(args):
        pc, qc = args
        C = pc.shape[0]
        qf = qc.astype(jnp.float32)
        own = pc // MOBA_BLOCK
        gate = jnp.einsum('bchd,bnhd->bchn', qf, k_mean_h)
        fully_past = jnp.arange(nb)[None, :] < own[:, None]
        gate = jnp.where(fully_past[None, :, None, :], gate, -jnp.inf)
        g_val, g_idx = lax.top_k(gate, MOBA_TOPK)
        own_b = jnp.broadcast_to(own[None, :, None, None], (B, C, N_HEADS_B, 1)).astype(g_idx.dtype)
        sel = jnp.concatenate([g_idx, own_b], axis=-1)
        sel_ok = jnp.concatenate([g_val > -jnp.inf, jnp.ones((B, C, N_HEADS_B, 1), bool)], axis=-1)
        kv_sel = blocks[bidx, sel, :, head_kv[None, None, :, None]]
        kpos = sel[..., None] * MOBA_BLOCK + offs
        ok = sel_ok[..., None] & (kpos <= pc[None, :, None, None, None])
        logits = jnp.einsum('bchd,bchnkd->bchnk', qf, kv_sel[..., 0, :].astype(jnp.float32)) * HEAD_DIM ** -0.5
        logits = jnp.where(ok, logits, -jnp.inf).reshape(B, C, N_HEADS_B, (MOBA_TOPK + 1) * MOBA_BLOCK)
        p = jax.nn.softmax(logits, axis=-1)
        v = kv_sel[..., 1, :].astype(jnp.float32).reshape(B, C, N_HEADS_B, (MOBA_TOPK + 1) * MOBA_BLOCK, HEAD_DIM)
        o = jnp.einsum('bchm,bchmd->bchd', p, v)
        return o.astype(qc.dtype)

    return map_query_chunks(one_chunk, qpos, [q], chunk)


def decoder_layer(x, pos, params, past):
    norm_g, w_in, qn_a, kn_a, qn_i, kn_i, qn_b, kn_b, w_ba, w_bb, w_out = params
    qa, kv_a, qi, ki, wi, za, qb, kv_b, zb, ga, gb = project(x, pos, norm_g, w_in, qn_a, kn_a, qn_i, kn_i, qn_b, kn_b)
    B, T, _ = x.shape
    if past is None:
        ki_all = ki
        kv_b_all = kv_b
        bidx = jnp.arange(B)[:, None, None]
        fetch_kv_a = lambda idx: kv_a[bidx, idx]
    else:
        kv_a_pool, idx_pool, kv_b_pool, page_table = past
        ki_all = jnp.concatenate([gather_past(idx_pool, page_table), ki], axis=1)
        kv_b_all = jnp.concatenate([gather_past(kv_b_pool, page_table), kv_b], axis=1)
        fetch_kv_a = lambda idx: fetch_paged_rows(kv_a_pool, kv_a, page_table, idx)
    oa = dsa_attention(qa, qi, wi, pos, ki_all, fetch_kv_a)
    ob = moba_attention(qb, pos, kv_b_all)
    a = jnp.einsum('bte,ed->btd', oa.reshape(B, T, WIDTH_A) * jax.nn.silu(za), w_ba)
    b = jnp.einsum('bte,ed->btd', ob.reshape(B, T, WIDTH_B) * jax.nn.silu(zb), w_bb)
    merged = jax.nn.sigmoid(ga) * a + jax.nn.sigmoid(gb) * b
    y = x + jnp.einsum('btd,de->bte', merged, w_out)
    return y, kv_a, ki, kv_b


def setup_inputs(seed: int = 0) -> dict:
    key = jax.random.key(seed)
    ks = jax.random.split(key, 20)
    f32 = jnp.float32
    n_pages = PAST_LEN // PAGE_SIZE
    used = DEC_BATCH * n_pages
    n_phys = used + max(1, used // 4)

    def gain(k, n):
        return 1.0 + 0.02 * jax.random.normal(k, (DEPTH, n), f32)

    def dense(k, fi, fo):
        return jax.random.normal(k, (DEPTH, fi, fo), f32) * fi ** -0.5

    page_table = jax.random.permutation(ks[5], n_phys)[:used].reshape(DEC_BATCH, n_pages).astype(jnp.int32)
    return {
        "x_prompt": jax.random.normal(ks[0], (BATCH, SEQ, D_MODEL), f32),
        "x_sample": jax.random.normal(ks[1], (DEC_BATCH, DEC_SEQ, D_MODEL), f32),
        "cache_kv_a": jax.random.normal(ks[2], (DEPTH, n_phys, PAGE_SIZE, N_KV_A, 2, HEAD_DIM), f32),
        "cache_idx_k": jax.random.normal(ks[3], (DEPTH, n_phys, PAGE_SIZE, IDX_DIM), f32),
        "cache_kv_b": jax.random.normal(ks[4], (DEPTH, n_phys, PAGE_SIZE, N_KV_B, 2, HEAD_DIM), f32),
        "page_table": page_table,
        "norm_g": gain(ks[6], D_MODEL),
        "w_in": dense(ks[7], D_MODEL, IN_WIDTH),
        "q_norm_a": gain(ks[8], HEAD_DIM),
        "k_norm_a": gain(ks[9], HEAD_DIM),
        "idx_q_norm": gain(ks[10], IDX_DIM),
        "idx_k_norm": gain(ks[11], IDX_DIM),
        "q_norm_b": gain(ks[12], HEAD_DIM),
        "k_norm_b": gain(ks[13], HEAD_DIM),
        "w_branch_a": dense(ks[14], WIDTH_A, D_MODEL),
        "w_branch_b": dense(ks[15], WIDTH_B, D_MODEL),
        "w_out": dense(ks[16], D_MODEL, D_MODEL),
    }


def reference(x_prompt, x_sample, cache_kv_a, cache_idx_k, cache_kv_b, page_table,
              norm_g, w_in, q_norm_a, k_norm_a, idx_q_norm, idx_k_norm, q_norm_b, k_norm_b,
              w_branch_a, w_branch_b, w_out):
    past_len = page_table.shape[1] * PAGE_SIZE
    pos_prompt = jnp.arange(x_prompt.shape[1], dtype=jnp.int32)
    pos_sample = past_len + jnp.arange(x_sample.shape[1], dtype=jnp.int32)
    hp, hs = x_prompt, x_sample
    kv_a_p, idx_p, kv_b_p, kv_a_s, idx_s, kv_b_s = [], [], [], [], [], []
    for l in range(DEPTH):
        params = (norm_g[l], w_in[l], q_norm_a[l], k_norm_a[l], idx_q_norm[l], idx_k_norm[l],
                  q_norm_b[l], k_norm_b[l], w_branch_a[l], w_branch_b[l], w_out[l])
        hp, ka, ki, kb = decoder_layer(hp, pos_prompt, params, None)
        kv_a_p.append(ka)
        idx_p.append(ki)
        kv_b_p.append(kb)
        hs, ka, ki, kb = decoder_layer(hs, pos_sample, params,
                                       (cache_kv_a[l], cache_idx_k[l], cache_kv_b[l], page_table))
        kv_a_s.append(ka)
        idx_s.append(ki)
        kv_b_s.append(kb)
    new_kv_a_prompt = jnp.stack(kv_a_p, axis=0)
    new_idx_k_prompt = jnp.stack(idx_p, axis=0)
    new_kv_b_prompt = jnp.stack(kv_b_p, axis=0)
    new_kv_a_sample = jnp.stack(kv_a_s, axis=0)
    new_idx_k_sample = jnp.stack(idx_s, axis=0)
    new_kv_b_sample = jnp.stack(kv_b_s, axis=0)
    return (hp, hs, new_kv_a_prompt, new_idx_k_prompt, new_kv_b_prompt,
            new_kv_a_sample, new_idx_k_sample, new_kv_b_sample)
```

```python
import functools

import numpy as np
import jax
import jax.numpy as jnp
from jax import lax
from jax.experimental import pallas as pl
from jax.experimental.pallas import tpu as pltpu

F32 = jnp.float32
BF16 = jnp.bfloat16

HEAD_DIM = 64
N_HEADS = 8
N_KV = 2
GROUP = N_HEADS // N_KV
WIDTH = N_HEADS * HEAD_DIM
KV_WIDTH = N_KV * 2 * HEAD_DIM
IDX_HEADS = 8
IDX_DIM = 64
IDX_TOPK = 256
MOBA_BLOCK = 256
MOBA_TOPK = 3
PAGE_SIZE = 128
ROPE_THETA = 500000.0
ROT_HALF = HEAD_DIM // 4 // 2
RMS_EPS = 1e-6
SM_SCALE = HEAD_DIM ** -0.5

LANES = 128
MXU_DIM = 256
VMEM_LIMIT_BYTES = 56 * 1024 * 1024

MASKED = -1e30
BISECT_STEPS = 6

_NT = (((1,), (1,)), ((), ()))


def _cparams(*sem):
    return pltpu.CompilerParams(dimension_semantics=sem, vmem_limit_bytes=VMEM_LIMIT_BYTES)


def _proj_kernel(x_ref, ng_ref, wq_ref, wk_ref, wg_ref, gq_ref, gka_ref, gkb_ref, bd_ref,
                 cq_ref, s1q_ref, s2q_ref, ck_ref, s1k_ref, s2k_ref,
                 q16_ref, kva_ref, kvb_ref, ki_ref, kva16_ref, kvb16_ref, kiw_ref, kiw16_ref, gate16_ref):
    x = x_ref[...]
    inv = lax.rsqrt(jnp.mean(x * x, axis=-1, keepdims=True) + RMS_EPS)
    h = (x * inv * ng_ref[...]).astype(BF16)

    def head_mean_sq(z):
        parts = []
        for c in range(0, z.shape[1], MXU_DIM):
            cw = min(MXU_DIM, z.shape[1] - c)
            zc = z[:, c:c + cw]
            parts.append(jnp.dot((zc * zc).astype(BF16), bd_ref[:cw, :cw], preferred_element_type=F32))
        return parts[0] if len(parts) == 1 else jnp.concatenate(parts, axis=1)

    def rope(y, c_ref, s1_ref, s2_ref):
        return (y * c_ref[...] + pltpu.roll(y, LANES - ROT_HALF, 1) * s1_ref[...]
                + pltpu.roll(y, ROT_HALF, 1) * s2_ref[...])

    qw = q16_ref.shape[1]
    for c0 in range(0, qw, 512):
        z = jnp.dot(h, wq_ref[:, c0:c0 + 512], preferred_element_type=F32)
        y = z * lax.rsqrt(head_mean_sq(z) + RMS_EPS) * gq_ref[:, c0:c0 + 512]
        for j in range(0, 512, LANES):
            q16_ref[:, c0 + j:c0 + j + LANES] = rope(y[:, j:j + LANES], cq_ref, s1q_ref, s2q_ref).astype(BF16)

    z = jnp.dot(h, wk_ref[...], preferred_element_type=F32)
    y = z * (lax.rsqrt(head_mean_sq(z) + RMS_EPS) * gka_ref[...] + gkb_ref[...])
    r = [rope(y[:, j * LANES:(j + 1) * LANES], ck_ref, s1k_ref, s2k_ref) for j in range(5)]
    for j in range(2):
        kva_ref[:, j * LANES:(j + 1) * LANES] = r[j]
        kva16_ref[:, j * LANES:(j + 1) * LANES] = r[j].astype(BF16)
        kvb_ref[:, j * LANES:(j + 1) * LANES] = r[2 + j]
        kvb16_ref[:, j * LANES:(j + 1) * LANES] = r[2 + j].astype(BF16)
    kiw_ref[...] = r[4]
    kiw16_ref[...] = r[4].astype(BF16)
    ki_ref[...] = r[4][:, :IDX_DIM]

    gw = gate16_ref.shape[1]
    n_silu = 2 * WIDTH
    for c0 in range(0, gw, 512):
        z = jnp.dot(h, wg_ref[:, c0:c0 + 512], preferred_element_type=F32)
        sig = 1.0 / (1.0 + jnp.exp(-z))
        gate16_ref[:, c0:c0 + 512] = (z * sig if c0 < n_silu else sig).astype(BF16)


def _const_spec(shape):
    return pl.BlockSpec(shape, lambda i: (0,) * len(shape))


def _project(x2d, tables, wts, tm):
    n, d = x2d.shape
    n_per = tables[0].shape[0] // tm
    wq, wk, wg = wts["wq"], wts["wk"], wts["wg"]
    row = lambda w: pl.BlockSpec((tm, w), lambda i: (i, 0))
    tab = pl.BlockSpec((tm, LANES), lambda i: (i % n_per, 0))
    in_specs = [row(d), _const_spec((1, d)), _const_spec(wq.shape), _const_spec(wk.shape), _const_spec(wg.shape),
                _const_spec((1, wq.shape[1])), _const_spec((1, wk.shape[1])), _const_spec((1, wk.shape[1])),
                _const_spec((MXU_DIM, MXU_DIM))] + [tab] * 6
    out_shape = [
        jax.ShapeDtypeStruct((n, wq.shape[1]), BF16),
        jax.ShapeDtypeStruct((n, KV_WIDTH), F32),
        jax.ShapeDtypeStruct((n, KV_WIDTH), F32),
        jax.ShapeDtypeStruct((n, IDX_DIM), F32),
        jax.ShapeDtypeStruct((n, KV_WIDTH), BF16),
        jax.ShapeDtypeStruct((n, KV_WIDTH), BF16),
        jax.ShapeDtypeStruct((n, LANES), F32),
        jax.ShapeDtypeStruct((n, LANES), BF16),
        jax.ShapeDtypeStruct((n, wg.shape[1]), BF16),
    ]
    out_specs = [row(s.shape[1]) for s in out_shape]
    outs = pl.pallas_call(
        _proj_kernel, grid=(n // tm,), in_specs=in_specs, out_specs=out_specs, out_shape=out_shape,
        compiler_params=_cparams("arbitrary"), name="project",
    )(x2d, wts["ng"], wq, wk, wg, wts["gq"], wts["gka"], wts["gkb"], wts["bd"], *tables)
    names = ("q16", "kva", "kvb", "ki", "kva16", "kvb16", "kiw", "kiw16", "gate16")
    return dict(zip(names, outs))


def _fold_lanes(x, op):
    acc = x[:, :LANES]
    for j in range(1, x.shape[1] // LANES):
        acc = op(acc, x[:, j * LANES:(j + 1) * LANES])
    return acc


def _select_threshold(score_ref, n_chunks, n_valid, k):
    _, rows, width = score_ref.shape
    kf = float(k)
    neg, pos = -jnp.inf, jnp.inf

    def reduce_all(fn, op, init, lane_reduce):
        def body(c, acc):
            return op(acc, _fold_lanes(fn(score_ref[c], c), op))
        acc = lax.fori_loop(0, n_chunks, body, jnp.full((rows, LANES), init, F32))
        return lane_reduce(acc, axis=1, keepdims=True)

    def count(pred):
        return reduce_all(lambda x, c: jnp.where(pred(x, c), 1.0, 0.0), jnp.add, 0.0, jnp.sum)

    small = n_valid <= k
    row_min = reduce_all(lambda x, c: jnp.where(x > neg, x, pos), jnp.minimum, pos, jnp.min)

    def cond(s):
        return jnp.sum(1.0 - s[3]) > 0.0

    def body(s):
        lo, hi, ans, done = s
        vmax = reduce_all(lambda x, c: jnp.where(x < hi, x, neg), jnp.maximum, neg, jnp.max)
        hit = count(lambda x, c: x >= vmax) >= kf
        ans = jnp.where(hit & (done < 0.5), vmax, ans)
        hi = jnp.where(hit, hi, vmax)
        done = jnp.where(hit, 1.0, done)
        for _ in range(BISECT_STEPS):
            mid = 0.5 * lo + 0.5 * hi
            ge = count(lambda x, c: x >= mid) >= kf
            lo = jnp.where(ge, mid, lo)
            hi = jnp.where(ge, hi, mid)
        return lo, hi, ans, done

    init = (row_min, jnp.full((rows, 1), pos, F32), jnp.full((rows, 1), neg, F32),
            jnp.where(small, 1.0, 0.0).astype(F32))
    thr = lax.while_loop(cond, body, init)[2]

    need = kf - count(lambda x, c: x > thr)
    n_eq = count(lambda x, c: x == thr)
    tie = jnp.logical_and(jnp.logical_not(small), n_eq > need)
    total = score_ref.shape[0] * width
    col0 = lax.broadcasted_iota(jnp.int32, (1, width), 1)

    def tie_search():
        lo = jnp.full((rows, 1), -1.0, F32)
        hi = jnp.full((rows, 1), float(total - 1), F32)
        for _ in range(int(np.ceil(np.log2(total))) + 1):
            mid = jnp.floor(0.5 * (lo + hi))
            ok = count(lambda x, c: jnp.logical_and(x == thr, (col0 + c * width).astype(F32) <= mid)) >= need
            hi = jnp.where(ok, mid, hi)
            lo = jnp.where(ok, lo, mid)
        return hi

    any_tie = jnp.sum(jnp.where(tie, 1.0, 0.0)) > 0.0
    jthr = lax.cond(any_tie, tie_search, lambda: jnp.full((rows, 1), float(total), F32))
    jthr = jnp.where(small, -1.0, jthr)
    return thr, jthr


def _flash_step(h, q, k, kv, sel, m_ref, l_ref, acc_ref):
    s = lax.dot_general(q, k, _NT, preferred_element_type=F32) * SM_SCALE
    s = jnp.where(sel, s, MASKED)
    m_old = m_ref[h]
    m_new = jnp.maximum(m_old, jnp.max(s, axis=1, keepdims=True))
    p = jnp.where(sel, jnp.exp(s - m_new), 0.0)
    alpha = jnp.exp(m_old - m_new)
    l_ref[h] = alpha * l_ref[h] + jnp.sum(p, axis=1, keepdims=True)
    acc_ref[h] = alpha * acc_ref[h] + jnp.dot(p.astype(BF16), kv, preferred_element_type=F32)
    m_ref[h] = m_new


def _flash_init(m_ref, l_ref, acc_ref):
    m_ref[...] = jnp.full(m_ref.shape, MASKED, F32)
    l_ref[...] = jnp.zeros(l_ref.shape, F32)
    acc_ref[...] = jnp.zeros(acc_ref.shape, F32)


def _flash_finish(o_ref, l_ref, acc_ref):
    outs = [acc_ref[h][:, HEAD_DIM:] / l_ref[h] for h in range(N_HEADS)]
    o_ref[...] = jnp.concatenate(outs, axis=1).astype(o_ref.dtype)


def _dsa_prompt_kernel(qa_ref, qi_ref, kiw_ref, kiw16_ref, kva16_ref, o_ref,
                       score_ref, m_ref, l_ref, acc_ref, *, k_sel):
    i = pl.program_id(1)
    n_total, tq, kc = score_ref.shape
    n_kc = (i * tq + tq + kc - 1) // kc
    rows = i * tq + lax.broadcasted_iota(jnp.int32, (tq, 1), 0)
    w = kiw_ref[:, IDX_DIM:IDX_DIM + IDX_HEADS]
    col0 = lax.broadcasted_iota(jnp.int32, (1, kc), 1)

    def idx_body(c, carry):
        k0 = pl.multiple_of(c * kc, kc)
        kblk = kiw16_ref[pl.ds(k0, kc), 0:IDX_DIM]
        acc = jnp.zeros((tq, kc), F32)
        for h in range(IDX_HEADS):
            s = lax.dot_general(qi_ref[:, h * IDX_DIM:(h + 1) * IDX_DIM], kblk, _NT, preferred_element_type=F32)
            acc = acc + jnp.maximum(s, 0.0) * w[:, h:h + 1]
        score_ref[c] = jnp.where(col0 + k0 <= rows, acc, -jnp.inf)
        return carry

    lax.fori_loop(0, n_kc, idx_body, 0)
    thr, jthr = _select_threshold(score_ref, n_kc, rows + 1, k_sel)

    _flash_init(m_ref, l_ref, acc_ref)

    def att_body(c, carry):
        k0 = pl.multiple_of(c * kc, kc)
        x = score_ref[c]
        sel = jnp.logical_or(x > thr, jnp.logical_and(x == thr, (col0 + k0).astype(F32) <= jthr))
        for g in range(N_KV):
            kv = kva16_ref[pl.ds(k0, kc), g * LANES:(g + 1) * LANES]
            k = kv[:, :HEAD_DIM]
            for r in range(GROUP):
                h = g * GROUP + r
                _flash_step(h, qa_ref[:, h * HEAD_DIM:(h + 1) * HEAD_DIM], k, kv, sel, m_ref, l_ref, acc_ref)
        return carry

    lax.fori_loop(0, n_kc, att_body, 0)
    _flash_finish(o_ref, l_ref, acc_ref)


def _dsa_prompt(pp, b, t, tq, kc):
    n_t = t // tq
    k_sel = min(IDX_TOPK, t // 4)
    kern = functools.partial(_dsa_prompt_kernel, k_sel=k_sel)
    return pl.pallas_call(
        kern, grid=(b, n_t),
        in_specs=[pl.BlockSpec((tq, WIDTH), lambda bb, i: (bb * n_t + i, 0)),
                  pl.BlockSpec((tq, WIDTH), lambda bb, i: (bb * n_t + i, 1)),
                  pl.BlockSpec((tq, LANES), lambda bb, i: (bb * n_t + i, 0)),
                  pl.BlockSpec((t, LANES), lambda bb, i: (bb, 0)),
                  pl.BlockSpec((t, KV_WIDTH), lambda bb, i: (bb, 0))],
        out_specs=pl.BlockSpec((tq, WIDTH), lambda bb, i: (bb * n_t + i, 0)),
        out_shape=jax.ShapeDtypeStruct((b * t, WIDTH), BF16),
        scratch_shapes=[pltpu.VMEM((t // kc, tq, kc), F32),
                        pltpu.VMEM((N_HEADS, tq, 1), F32), pltpu.VMEM((N_HEADS, tq, 1), F32),
                        pltpu.VMEM((N_HEADS, tq, LANES), F32)],
        compiler_params=_cparams("arbitrary", "arbitrary"), name="dsa_prompt",
    )(pp["q16"], pp["q16"], pp["kiw"], pp["kiw16"], pp["kva16"])


def _top_blocks(gate, valid):
    lane = lax.broadcasted_iota(jnp.int32, gate.shape, 1).astype(F32)
    g = jnp.where(valid, gate, -jnp.inf)
    sel = jnp.zeros(gate.shape, F32)
    for _ in range(MOBA_TOPK):
        m = jnp.max(g, axis=1, keepdims=True)
        idx = jnp.min(jnp.where(g == m, lane, float(gate.shape[1])), axis=1, keepdims=True)
        pick = lane == idx
        sel = jnp.where(jnp.logical_and(pick, m > -jnp.inf), 1.0, sel)
        g = jnp.where(pick, -jnp.inf, g)
    return sel


def _moba_prompt_kernel(qb_ref, kvb_ref, kvb16_ref, o_ref, kmean_ref, sel_ref, m_ref, l_ref, acc_ref):
    own = pl.program_id(1)
    t = kvb_ref.shape[0]
    tq = qb_ref.shape[0]
    nb = t // MOBA_BLOCK

    @pl.when(own == 0)
    def _():
        kmean_ref[...] = jnp.zeros(kmean_ref.shape, F32)
        for n in range(nb):
            blk = kvb_ref[n * MOBA_BLOCK:(n + 1) * MOBA_BLOCK, :]
            kmean_ref[n:n + 1, :] = jnp.mean(blk, axis=0, keepdims=True)

    lane = lax.broadcasted_iota(jnp.int32, (tq, LANES), 1)
    for h in range(N_HEADS):
        g = h // GROUP
        km = kmean_ref[:, g * LANES:g * LANES + HEAD_DIM].astype(BF16)
        gate = lax.dot_general(qb_ref[:, h * HEAD_DIM:(h + 1) * HEAD_DIM], km, _NT, preferred_element_type=F32)
        sel_ref[h] = _top_blocks(gate, lane < own)

    _flash_init(m_ref, l_ref, acc_ref)

    def blk_body(n, carry):
        k0 = pl.multiple_of(n * MOBA_BLOCK, MOBA_BLOCK)
        for g in range(N_KV):
            kv = kvb16_ref[pl.ds(k0, MOBA_BLOCK), g * LANES:(g + 1) * LANES]
            k = kv[:, :HEAD_DIM]
            for r in range(GROUP):
                h = g * GROUP + r
                picked = jnp.max(jnp.where(lane == n, sel_ref[h], 0.0), axis=1, keepdims=True) > 0.5
                sel = jnp.broadcast_to(picked, (tq, MOBA_BLOCK))
                _flash_step(h, qb_ref[:, h * HEAD_DIM:(h + 1) * HEAD_DIM], k, kv, sel, m_ref, l_ref, acc_ref)
        return carry

    lax.fori_loop(0, own, blk_body, 0)

    k0 = pl.multiple_of(own * MOBA_BLOCK, MOBA_BLOCK)
    causal = (lax.broadcasted_iota(jnp.int32, (tq, MOBA_BLOCK), 1)
              <= lax.broadcasted_iota(jnp.int32, (tq, MOBA_BLOCK), 0))
    for g in range(N_KV):
        kv = kvb16_ref[pl.ds(k0, MOBA_BLOCK), g * LANES:(g + 1) * LANES]
        k = kv[:, :HEAD_DIM]
        for r in range(GROUP):
            h = g * GROUP + r
            _flash_step(h, qb_ref[:, h * HEAD_DIM:(h + 1) * HEAD_DIM], k, kv, causal, m_ref, l_ref, acc_ref)
    _flash_finish(o_ref, l_ref, acc_ref)


def _moba_prompt(pp, b, t):
    tq = MOBA_BLOCK
    n_t = t // tq
    assert t % MOBA_BLOCK == 0 and n_t <= LANES
    return pl.pallas_call(
        _moba_prompt_kernel, grid=(b, n_t),
        in_specs=[pl.BlockSpec((tq, WIDTH), lambda bb, i: (bb * n_t + i, 2)),
                  pl.BlockSpec((t, KV_WIDTH), lambda bb, i: (bb, 0)),
                  pl.BlockSpec((t, KV_WIDTH), lambda bb, i: (bb, 0))],
        out_specs=pl.BlockSpec((tq, WIDTH), lambda bb, i: (bb * n_t + i, 0)),
        out_shape=jax.ShapeDtypeStruct((b * t, WIDTH), BF16),
        scratch_shapes=[pltpu.VMEM((LANES, KV_WIDTH), F32),
                        pltpu.VMEM((N_HEADS, tq, LANES), F32),
                        pltpu.VMEM((N_HEADS, tq, 1), F32), pltpu.VMEM((N_HEADS, tq, 1), F32),
                        pltpu.VMEM((N_HEADS, tq, LANES), F32)],
        compiler_params=_cparams("arbitrary", "arbitrary"), name="moba_prompt",
    )(pp["q16"], pp["kvb"], pp["kvb16"])


def _page_copy(pt_ref, pool, buf, sem, b, slot, p):
    return pltpu.make_async_copy(pool.at[pt_ref[b, p]], buf.at[slot, pl.ds(p * PAGE_SIZE, PAGE_SIZE)], sem.at[slot])


def _paged_fetch(pt_ref, pools, bufs, sems, past):
    b = pl.program_id(0)
    nb = pl.num_programs(0)
    slot = b % 2
    n_pages = past // PAGE_SIZE

    def start(bb, sl):
        def body(p, carry):
            for pool, buf, sem in zip(pools, bufs, sems):
                _page_copy(pt_ref, pool, buf, sem, bb, sl, p).start()
            return carry
        lax.fori_loop(0, n_pages, body, 0)

    @pl.when(b == 0)
    def _():
        for buf in bufs:
            for sl in range(2):
                buf[sl, past:, :] = jnp.zeros((buf.shape[1] - past, buf.shape[2]), buf.dtype)
        start(0, 0)

    @pl.when(b + 1 < nb)
    def _():
        start(b + 1, 1 - slot)

    def wait_body(p, carry):
        for pool, buf, sem in zip(pools, bufs, sems):
            _page_copy(pt_ref, pool, buf, sem, b, slot, p).wait()
        return carry
    lax.fori_loop(0, n_pages, wait_body, 0)
    return slot


def _stack_heads(q, heads):
    qf = q.astype(F32)
    return jnp.concatenate([qf[:, h * HEAD_DIM:(h + 1) * HEAD_DIM] for h in heads], axis=0).astype(BF16)


def _dense_masked_attention(q, kv, sel):
    s = lax.dot_general(q, kv[:, :HEAD_DIM], _NT, preferred_element_type=F32) * SM_SCALE
    s = jnp.where(sel, s, MASKED)
    m = jnp.max(s, axis=1, keepdims=True)
    p = jnp.where(sel, jnp.exp(s - m), 0.0)
    l = jnp.sum(p, axis=1, keepdims=True)
    o = jnp.dot(p.astype(BF16), kv, preferred_element_type=F32)
    return o[:, HEAD_DIM:] / l


def _store_heads(o_ref, per_group, nq):
    outs = [per_group[g][r * nq:(r + 1) * nq, :] for g in range(N_KV) for r in range(GROUP)]
    o_ref[0] = jnp.concatenate(outs, axis=1).astype(o_ref.dtype)


def _dsa_decode_kernel(pt_ref, q_ref, kiw_ref, kva_ref, idx_pool, kv_pool, o_ref,
                       idxbuf, kvbuf, score_ref, sems, *, past, k_sel):
    nq = q_ref.shape[1]
    lp = idxbuf.shape[1]
    slot = _paged_fetch(pt_ref, (idx_pool, kv_pool), (idxbuf, kvbuf), (sems.at[0], sems.at[1]), past)
    kiw = kiw_ref[0]
    idxbuf[slot, past:past + nq, :] = kiw[:, :IDX_DIM]
    kvbuf[slot, past:past + nq, :] = kva_ref[0]

    q = q_ref[0]
    qa, qi = q[:, :WIDTH], q[:, WIDTH:2 * WIDTH]
    keys = idxbuf[slot].astype(BF16)
    s = lax.dot_general(_stack_heads(qi, range(IDX_HEADS)), keys, _NT, preferred_element_type=F32)
    score = jnp.zeros((nq, lp), F32)
    for h in range(IDX_HEADS):
        score = score + jnp.maximum(s[h * nq:(h + 1) * nq, :], 0.0) * kiw[:, IDX_DIM + h:IDX_DIM + h + 1]
    col = lax.broadcasted_iota(jnp.int32, (1, lp), 1)
    qpos = past + lax.broadcasted_iota(jnp.int32, (nq, 1), 0)
    score_ref[0] = jnp.where(col <= qpos, score, -jnp.inf)
    thr, jthr = _select_threshold(score_ref, 1, qpos + 1, k_sel)
    x = score_ref[0]
    sel = jnp.logical_or(x > thr, jnp.logical_and(x == thr, col.astype(F32) <= jthr))
    sel = jnp.concatenate([sel] * GROUP, axis=0)

    outs = []
    for g in range(N_KV):
        kv = kvbuf[slot, :, g * LANES:(g + 1) * LANES].astype(BF16)
        qg = _stack_heads(qa, range(g * GROUP, (g + 1) * GROUP))
        outs.append(_dense_masked_attention(qg, kv, sel))
    _store_heads(o_ref, outs, nq)


def _moba_decode_kernel(pt_ref, q_ref, kvb_ref, kv_pool, o_ref, kvbuf, sems, *, past):
    nq = q_ref.shape[1]
    lp = kvbuf.shape[1]
    n_blk = past // MOBA_BLOCK
    slot = _paged_fetch(pt_ref, (kv_pool,), (kvbuf,), (sems.at[0],), past)
    kvbuf[slot, past:past + nq, :] = kvb_ref[0]
    qb = q_ref[0][:, 2 * WIDTH:3 * WIDTH]

    rows = GROUP * nq
    col = lax.broadcasted_iota(jnp.int32, (rows, lp - past), 1)
    qidx = lax.broadcasted_iota(jnp.int32, (rows, lp - past), 0) % nq
    sel_new = col <= qidx
    jblk = lax.broadcasted_iota(jnp.int32, (n_blk, past), 1) // MOBA_BLOCK
    expand = jnp.where(jblk == lax.broadcasted_iota(jnp.int32, (n_blk, past), 0), 1.0, 0.0).astype(BF16)

    outs = []
    for g in range(N_KV):
        kvf = kvbuf[slot, :, g * LANES:(g + 1) * LANES]
        kmean = jnp.mean(kvf[:past].reshape(n_blk, MOBA_BLOCK, LANES), axis=1)
        qg = _stack_heads(qb, range(g * GROUP, (g + 1) * GROUP))
        gate = lax.dot_general(qg, kmean[:, :HEAD_DIM].astype(BF16), _NT, preferred_element_type=F32)
        picked = _top_blocks(gate, jnp.full(gate.shape, True))
        sel_past = jnp.dot(picked.astype(BF16), expand, preferred_element_type=F32) > 0.5
        sel = jnp.concatenate([sel_past, sel_new], axis=1)
        outs.append(_dense_masked_attention(qg, kvf.astype(BF16), sel))
    _store_heads(o_ref, outs, nq)


def _sample_specs(db, nq, widths_blocks):
    return [pl.BlockSpec((1, nq, w), (lambda bb, pt, j=j: (bb, 0, j))) for w, j in widths_blocks]


def _dsa_decode(ps, page_table, idx_pool, kv_pool, db, nq):
    past = page_table.shape[1] * PAGE_SIZE
    lp = past + LANES
    k_sel = min(IDX_TOPK, (past + nq) // 4)
    qw = ps["q16"].shape[1]
    kern = functools.partial(_dsa_decode_kernel, past=past, k_sel=k_sel)
    grid_spec = pltpu.PrefetchScalarGridSpec(
        num_scalar_prefetch=1, grid=(db,),
        in_specs=_sample_specs(db, nq, [(qw, 0), (LANES, 0), (KV_WIDTH, 0)])
        + [pl.BlockSpec(memory_space=pl.ANY), pl.BlockSpec(memory_space=pl.ANY)],
        out_specs=pl.BlockSpec((1, nq, WIDTH), lambda bb, pt: (bb, 0, 0)),
        scratch_shapes=[pltpu.VMEM((2, lp, IDX_DIM), F32), pltpu.VMEM((2, lp, KV_WIDTH), F32),
                        pltpu.VMEM((1, nq, lp), F32), pltpu.SemaphoreType.DMA((2, 2))])
    return pl.pallas_call(
        kern, grid_spec=grid_spec, out_shape=jax.ShapeDtypeStruct((db, nq, WIDTH), BF16),
        compiler_params=_cparams("arbitrary"), name="dsa_decode",
    )(page_table, ps["q16"].reshape(db, nq, qw), ps["kiw"].reshape(db, nq, LANES),
      ps["kva"].reshape(db, nq, KV_WIDTH), idx_pool, kv_pool)


def _moba_decode(ps, page_table, kv_pool, db, nq):
    past = page_table.shape[1] * PAGE_SIZE
    assert past % MOBA_BLOCK == 0 and nq <= MOBA_BLOCK
    lp = past + LANES
    qw = ps["q16"].shape[1]
    kern = functools.partial(_moba_decode_kernel, past=past)
    grid_spec = pltpu.PrefetchScalarGridSpec(
        num_scalar_prefetch=1, grid=(db,),
        in_specs=_sample_specs(db, nq, [(qw, 0), (KV_WIDTH, 0)]) + [pl.BlockSpec(memory_space=pl.ANY)],
        out_specs=pl.BlockSpec((1, nq, WIDTH), lambda bb, pt: (bb, 0, 0)),
        scratch_shapes=[pltpu.VMEM((2, lp, KV_WIDTH), F32), pltpu.SemaphoreType.DMA((1, 2))])
    return pl.pallas_call(
        kern, grid_spec=grid_spec, out_shape=jax.ShapeDtypeStruct((db, nq, WIDTH), BF16),
        compiler_params=_cparams("arbitrary"), name="moba_decode",
    )(page_table, ps["q16"].reshape(db, nq, qw), ps["kvb"].reshape(db, nq, KV_WIDTH), kv_pool)


def _out_kernel(x_ref, oa_ref, ob_ref, gate_ref, wba_ref, wbb_ref, wo_ref, y_ref):
    d = x_ref.shape[1]
    ua = oa_ref[...] * gate_ref[:, 0:WIDTH]
    ub = ob_ref[...] * gate_ref[:, WIDTH:2 * WIDTH]
    a = jnp.dot(ua, wba_ref[...], preferred_element_type=F32)
    b = jnp.dot(ub, wbb_ref[...], preferred_element_type=F32)
    ga = gate_ref[:, 2 * WIDTH:2 * WIDTH + d].astype(F32)
    gb = gate_ref[:, 2 * WIDTH + d:2 * WIDTH + 2 * d].astype(F32)
    merged = (ga * a + gb * b).astype(BF16)
    y_ref[...] = x_ref[...] + jnp.dot(merged, wo_ref[...], preferred_element_type=F32)


def _out_proj(x2d, oa, ob, gate16, wts, tm):
    n, d = x2d.shape
    row = lambda w: pl.BlockSpec((tm, w), lambda i: (i, 0))
    return pl.pallas_call(
        _out_kernel, grid=(n // tm,),
        in_specs=[row(d), row(WIDTH), row(WIDTH), row(gate16.shape[1]),
                  _const_spec((WIDTH, d)), _const_spec((WIDTH, d)), _const_spec((d, d))],
        out_specs=row(d), out_shape=jax.ShapeDtypeStruct((n, d), F32),
        compiler_params=_cparams("arbitrary"), name="out_proj",
    )(x2d, oa, ob, gate16, wts["wba"], wts["wbb"], wts["wo"])


def _rope_tables(pos):
    p = pos.shape[0]
    inv = ROPE_THETA ** (-jnp.arange(ROT_HALF, dtype=F32) / ROT_HALF)
    ang = pos.astype(F32)[:, None] * inv[None, :]
    c, s = jnp.cos(ang), jnp.sin(ang)
    rest = HEAD_DIM - 2 * ROT_HALF
    z8, zr = jnp.zeros((p, ROT_HALF), F32), jnp.zeros((p, rest), F32)
    c_head = jnp.concatenate([c, c, jnp.ones((p, rest), F32)], axis=1)
    s1_head = jnp.concatenate([-s, z8, zr], axis=1)
    s2_head = jnp.concatenate([z8, s, zr], axis=1)
    one, zero = jnp.ones((p, HEAD_DIM), F32), jnp.zeros((p, HEAD_DIM), F32)
    cat = lambda a, b: jnp.concatenate([a, b], axis=1)
    return (cat(c_head, c_head), cat(s1_head, s1_head), cat(s2_head, s2_head),
            cat(c_head, one), cat(s1_head, zero), cat(s2_head, zero))


def _prep_weights(norm_g, w_in, qn_a, kn_a, qn_i, kn_i, qn_b, kn_b, w_ba, w_bb, w_out):
    d = w_in.shape[0]
    kvw = N_KV * HEAD_DIM
    splits = (WIDTH, kvw, kvw, IDX_HEADS * IDX_DIM, IDX_DIM, IDX_HEADS, WIDTH, WIDTH, kvw, kvw, WIDTH, d, d)
    offs = np.concatenate([[0], np.cumsum(splits)])
    qa, ka, va, qi, ki, wi, za, qb, kb, vb, zb, ga, gb = [w_in[:, offs[j]:offs[j + 1]] for j in range(13)]
    hd = HEAD_DIM
    pad = jnp.zeros((d, LANES - IDX_DIM - IDX_HEADS), F32)
    wk = jnp.concatenate([ka[:, :hd], va[:, :hd], ka[:, hd:], va[:, hd:],
                          kb[:, :hd], vb[:, :hd], kb[:, hd:], vb[:, hd:], ki, wi, pad], axis=1)
    zero = jnp.zeros((hd,), F32)
    one = jnp.ones((hd,), F32)
    wi_gain = jnp.concatenate([jnp.full((IDX_HEADS,), (IDX_HEADS * IDX_DIM) ** -0.5, F32),
                               jnp.zeros((hd - IDX_HEADS,), F32)])
    gka = jnp.concatenate([kn_a, zero, kn_a, zero, kn_b, zero, kn_b, zero, kn_i, zero])[None, :]
    gkb = jnp.concatenate([zero, one, zero, one, zero, one, zero, one, zero, wi_gain])[None, :]
    gq = jnp.concatenate([jnp.tile(qn_a, N_HEADS), jnp.tile(qn_i, IDX_HEADS), jnp.tile(qn_b, N_HEADS)])[None, :]
    blk = np.arange(MXU_DIM) // HEAD_DIM
    bd = jnp.asarray((blk[:, None] == blk[None, :]) / HEAD_DIM, BF16)
    return dict(
        ng=norm_g[None, :].astype(F32),
        wq=jnp.concatenate([qa, qi, qb], axis=1).astype(BF16), wk=wk.astype(BF16),
        wg=jnp.concatenate([za, zb, ga, gb], axis=1).astype(BF16),
        gq=gq, gka=gka, gkb=gkb, bd=bd,
        wba=w_ba.astype(BF16), wbb=w_bb.astype(BF16), wo=w_out.astype(BF16))


def kernel(x_prompt, x_sample, cache_kv_a, cache_idx_k, cache_kv_b, page_table, norm_g, w_in, q_norm_a, k_norm_a,
           idx_q_norm, idx_k_norm, q_norm_b, k_norm_b, w_branch_a, w_branch_b, w_out):
    b, t, d = x_prompt.shape
    db, nq, _ = x_sample.shape
    depth = w_in.shape[0]
    n_phys = cache_kv_a.shape[1]
    past = page_table.shape[1] * PAGE_SIZE
    tm = 256
    tm_s = min(tm, db * nq)
    assert t % tm == 0 and tm_s % nq == 0 and (db * nq) % tm_s == 0

    tab_p = _rope_tables(jnp.arange(t, dtype=jnp.int32))
    tab_s = tuple(jnp.tile(a, (tm_s // nq, 1)) for a in _rope_tables(past + jnp.arange(nq, dtype=jnp.int32)))

    hp = x_prompt.reshape(b * t, d)
    hs = x_sample.reshape(db * nq, d)
    new = [[] for _ in range(6)]
    for l in range(depth):
        wts = _prep_weights(norm_g[l], w_in[l], q_norm_a[l], k_norm_a[l], idx_q_norm[l], idx_k_norm[l],
                            q_norm_b[l], k_norm_b[l], w_branch_a[l], w_branch_b[l], w_out[l])
        pp = _project(hp, tab_p, wts, tm)
        oa = _dsa_prompt(pp, b, t, tq=256, kc=min(512, t))
        ob = _moba_prompt(pp, b, t)
        hp = _out_proj(hp, oa, ob, pp["gate16"], wts, tm)

        ps = _project(hs, tab_s, wts, tm_s)
        oa_s = _dsa_decode(ps, page_table, cache_idx_k[l], cache_kv_a[l].reshape(n_phys, PAGE_SIZE, KV_WIDTH), db, nq)
        ob_s = _moba_decode(ps, page_table, cache_kv_b[l].reshape(n_phys, PAGE_SIZE, KV_WIDTH), db, nq)
        hs = _out_proj(hs, oa_s.reshape(db * nq, WIDTH), ob_s.reshape(db * nq, WIDTH), ps["gate16"], wts, tm_s)

        new[0].append(pp["kva"].reshape(b, t, N_KV, 2, HEAD_DIM))
        new[1].append(pp["ki"].reshape(b, t, IDX_DIM))
        new[2].append(pp["kvb"].reshape(b, t, N_KV, 2, HEAD_DIM))
        new[3].append(ps["kva"].reshape(db, nq, N_KV, 2, HEAD_DIM))
        new[4].append(ps["ki"].reshape(db, nq, IDX_DIM))
        new[5].append(ps["kvb"].reshape(db, nq, N_KV, 2, HEAD_DIM))
    return (hp.reshape(b, t, d), hs.reshape(db, nq, d)) + tuple(jnp.stack(a, axis=0) for a in new)
```

```python
import functools

import numpy as np
import jax
import jax.numpy as jnp
from jax import lax
from jax.experimental import pallas as pl
from jax.experimental.pallas import tpu as pltpu

F32 = jnp.float32
BF16 = jnp.bfloat16

HEAD_DIM = 64
N_HEADS = 8
N_KV = 2
GROUP = N_HEADS // N_KV
WIDTH = N_HEADS * HEAD_DIM
KV_WIDTH = N_KV * 2 * HEAD_DIM
KV_SLABS = N_KV * 2
IDX_HEADS = 8
IDX_DIM = 64
IDX_TOPK = 256
MOBA_BLOCK = 256
MOBA_TOPK = 3
PAGE_SIZE = 128
ROPE_THETA = 500000.0
ROT_HALF = HEAD_DIM // 4 // 2
RMS_EPS = 1e-6
SM_SCALE = HEAD_DIM ** -0.5

LANES = 128
SUBLANES = 8
MXU_DIM = 256
BF16_ROWS = 16
VMEM_LIMIT_BYTES = 56 * 1024 * 1024

MASKED = -1e30
BISECT_STEPS = 6

_NT = (((1,), (1,)), ((), ()))


def _cparams(*sem):
    return pltpu.CompilerParams(dimension_semantics=sem, vmem_limit_bytes=VMEM_LIMIT_BYTES)


def _normed_input(x_ref, ng_ref):
    x = x_ref[...]
    inv = lax.rsqrt(jnp.mean(x * x, axis=-1, keepdims=True) + RMS_EPS)
    return (x * inv * ng_ref[...]).astype(BF16)


def _head_mean_sq(z, bd_ref):
    parts = []
    for c in range(0, z.shape[1], MXU_DIM):
        cw = min(MXU_DIM, z.shape[1] - c)
        zc = z[:, c:c + cw]
        parts.append(jnp.dot((zc * zc).astype(BF16), bd_ref[:cw, :cw], preferred_element_type=F32))
    return parts[0] if len(parts) == 1 else jnp.concatenate(parts, axis=1)


def _rope_lanes(y, c_ref, s1_ref, s2_ref):
    return (y * c_ref[...] + pltpu.roll(y, LANES - ROT_HALF, 1) * s1_ref[...]
            + pltpu.roll(y, ROT_HALF, 1) * s2_ref[...])


def _queries_and_gates(h, wq_ref, wg_ref, gq_ref, bd_ref, cq_ref, s1q_ref, s2q_ref, q16_ref, gate16_ref):
    for c0 in range(0, q16_ref.shape[1], 512):
        z = jnp.dot(h, wq_ref[:, c0:c0 + 512], preferred_element_type=F32)
        y = z * lax.rsqrt(_head_mean_sq(z, bd_ref) + RMS_EPS) * gq_ref[:, c0:c0 + 512]
        for j in range(0, 512, LANES):
            q16_ref[:, c0 + j:c0 + j + LANES] = _rope_lanes(y[:, j:j + LANES], cq_ref, s1q_ref, s2q_ref).astype(BF16)
    n_silu = 2 * WIDTH
    for c0 in range(0, gate16_ref.shape[1], 512):
        z = jnp.dot(h, wg_ref[:, c0:c0 + 512], preferred_element_type=F32)
        sig = 1.0 / (1.0 + jnp.exp(-z))
        gate16_ref[:, c0:c0 + 512] = (z * sig if c0 < n_silu else sig).astype(BF16)


def _proj_sample_kernel(x_ref, ng_ref, wq_ref, wk_ref, wg_ref, gq_ref, gka_ref, gkb_ref, bd_ref,
                        cq_ref, s1q_ref, s2q_ref, ck_ref, s1k_ref, s2k_ref,
                        q16_ref, gate16_ref, kva_ref, kvb_ref, kiw_ref):
    h = _normed_input(x_ref, ng_ref)
    _queries_and_gates(h, wq_ref, wg_ref, gq_ref, bd_ref, cq_ref, s1q_ref, s2q_ref, q16_ref, gate16_ref)
    z = jnp.dot(h, wk_ref[...], preferred_element_type=F32)
    y = z * (lax.rsqrt(_head_mean_sq(z, bd_ref) + RMS_EPS) * gka_ref[...] + gkb_ref[...])
    r = [_rope_lanes(y[:, j * LANES:(j + 1) * LANES], ck_ref, s1k_ref, s2k_ref) for j in range(5)]
    for j in range(2):
        kva_ref[:, j * LANES:(j + 1) * LANES] = r[j]
        kvb_ref[:, j * LANES:(j + 1) * LANES] = r[2 + j]
    kiw_ref[...] = r[4]


def _proj_prompt_kernel(x_ref, ng_ref, wq_ref, wkt_ref, wwi_ref, wg_ref, gq_ref, gkt_ref, bd_ref,
                        cq_ref, s1q_ref, s2q_ref, ct_ref, st_ref,
                        q16_ref, gate16_ref, wi_ref, kvat_ref, kvbt_ref, kit_ref, kvat16_ref, kvbt16_ref, kit16_ref):
    h = _normed_input(x_ref, ng_ref)
    _queries_and_gates(h, wq_ref, wg_ref, gq_ref, bd_ref, cq_ref, s1q_ref, s2q_ref, q16_ref, gate16_ref)
    wi_ref[...] = jnp.dot(h, wwi_ref[...], preferred_element_type=F32) * (IDX_HEADS * IDX_DIM) ** -0.5

    zt = lax.dot_general(wkt_ref[...], h, _NT, preferred_element_type=F32)
    cos, sin = ct_ref[...], st_ref[...]

    def key_head(j):
        z = zt[j * HEAD_DIM:(j + 1) * HEAD_DIM, :]
        y = z * lax.rsqrt(jnp.mean(z * z, axis=0, keepdims=True) + RMS_EPS) * gkt_ref[j * HEAD_DIM:(j + 1) * HEAD_DIM, :]
        y1, y2 = y[:ROT_HALF], y[ROT_HALF:2 * ROT_HALF]
        return jnp.concatenate([y1 * cos - y2 * sin, y2 * cos + y1 * sin, y[2 * ROT_HALF:]], axis=0)

    for j in range(KV_SLABS):
        for base, out_ref, out16_ref in ((0, kvat_ref, kvat16_ref), (KV_SLABS, kvbt_ref, kvbt16_ref)):
            slab = key_head(base + j) if j % 2 == 0 else zt[(base + j) * HEAD_DIM:(base + j + 1) * HEAD_DIM, :]
            out_ref[0, j] = slab
            out16_ref[0, j] = slab.astype(BF16)
    ki = key_head(2 * KV_SLABS)
    kit_ref[0] = ki
    kit16_ref[0] = ki.astype(BF16)


def _const_spec(shape, n_grid=1):
    if n_grid == 1:
        return pl.BlockSpec(shape, lambda i: (0,) * len(shape))
    return pl.BlockSpec(shape, lambda b, i: (0,) * len(shape))


def _project_sample(x2d, tables, wts, tm):
    n, d = x2d.shape
    wq, wk, wg = wts["wq"], wts["wk"], wts["wg"]
    row = lambda w: pl.BlockSpec((tm, w), lambda i: (i, 0))
    tab = pl.BlockSpec((tm, LANES), lambda i: (0, 0))
    in_specs = [row(d), _const_spec((1, d)), _const_spec(wq.shape), _const_spec(wk.shape), _const_spec(wg.shape),
                _const_spec((1, wq.shape[1])), _const_spec((1, wk.shape[1])), _const_spec((1, wk.shape[1])),
                _const_spec((MXU_DIM, MXU_DIM))] + [tab] * 6
    out_shape = [
        jax.ShapeDtypeStruct((n, wq.shape[1]), BF16),
        jax.ShapeDtypeStruct((n, wg.shape[1]), BF16),
        jax.ShapeDtypeStruct((n, KV_WIDTH), F32),
        jax.ShapeDtypeStruct((n, KV_WIDTH), F32),
        jax.ShapeDtypeStruct((n, LANES), F32),
    ]
    outs = pl.pallas_call(
        _proj_sample_kernel, grid=(n // tm,), in_specs=in_specs, out_specs=[row(s.shape[1]) for s in out_shape],
        out_shape=out_shape, compiler_params=_cparams("arbitrary"), name="project_sample",
    )(x2d, wts["ng"], wq, wk, wg, wts["gq"], wts["gka"], wts["gkb"], wts["bd"], *tables)
    return dict(zip(("q16", "gate16", "kva", "kvb", "kiw"), outs))


def _project_prompt(x2d, tables, wts, b, t, tm):
    n, d = x2d.shape
    n_t = t // tm
    wq, wkt, wwi, wg = wts["wq"], wts["wkt"], wts["wwi"], wts["wg"]
    row = lambda w: pl.BlockSpec((tm, w), lambda bb, i: (bb * n_t + i, 0))
    tab = pl.BlockSpec((tm, LANES), lambda bb, i: (i, 0))
    tab_t = pl.BlockSpec((ROT_HALF, tm), lambda bb, i: (0, i))
    cs = lambda shape: _const_spec(shape, 2)
    in_specs = [row(d), cs((1, d)), cs(wq.shape), cs(wkt.shape), cs(wwi.shape), cs(wg.shape),
                cs((1, wq.shape[1])), cs((wkt.shape[0], 1)), cs((MXU_DIM, MXU_DIM)), tab, tab, tab, tab_t, tab_t]
    kv_t = lambda dt: jax.ShapeDtypeStruct((b, KV_SLABS, HEAD_DIM, t), dt)
    ki_t = lambda dt: jax.ShapeDtypeStruct((b, IDX_DIM, t), dt)
    out_shape = [jax.ShapeDtypeStruct((n, wq.shape[1]), BF16), jax.ShapeDtypeStruct((n, wg.shape[1]), BF16),
                 jax.ShapeDtypeStruct((n, LANES), F32),
                 kv_t(F32), kv_t(F32), ki_t(F32), kv_t(BF16), kv_t(BF16), ki_t(BF16)]
    kv_spec = pl.BlockSpec((1, KV_SLABS, HEAD_DIM, tm), lambda bb, i: (bb, 0, 0, i))
    ki_spec = pl.BlockSpec((1, IDX_DIM, tm), lambda bb, i: (bb, 0, i))
    out_specs = [row(wq.shape[1]), row(wg.shape[1]), row(LANES), kv_spec, kv_spec, ki_spec, kv_spec, kv_spec, ki_spec]
    outs = pl.pallas_call(
        _proj_prompt_kernel, grid=(b, n_t), in_specs=in_specs, out_specs=out_specs, out_shape=out_shape,
        compiler_params=_cparams("arbitrary", "arbitrary"), name="project_prompt",
    )(x2d, wts["ng"], wq, wkt, wwi, wg, wts["gq"], wts["gkt"], wts["bd"], *tables)
    return dict(zip(("q16", "gate16", "wi", "kvat", "kvbt", "kit", "kvat16", "kvbt16", "kit16"), outs))


def _fold_lanes(x, op):
    acc = x[:, :LANES]
    for j in range(1, x.shape[1] // LANES):
        acc = op(acc, x[:, j * LANES:(j + 1) * LANES])
    return acc


def _fold_sublanes(x, op):
    n = x.shape[0] // SUBLANES
    parts = [x[j * SUBLANES:(j + 1) * SUBLANES, :] for j in range(min(n, SUBLANES))]
    for j in range(len(parts), n):
        parts[j % SUBLANES] = op(parts[j % SUBLANES], x[j * SUBLANES:(j + 1) * SUBLANES, :])
    while len(parts) > 1:
        parts = [op(parts[j], parts[j + 1]) for j in range(0, len(parts) - 1, 2)] + parts[len(parts) & ~1:]
    return parts[0]


def _select_threshold(score_ref, n_chunks, n_valid, k, key_axis):
    width = score_ref.shape[1 + key_axis]
    rows = score_ref.shape[2 - key_axis]
    kf = float(k)
    neg, pos = -jnp.inf, jnp.inf
    fold = _fold_lanes if key_axis == 1 else _fold_sublanes
    acc_shape = (rows, LANES) if key_axis == 1 else (SUBLANES, rows)

    def reduce_all(fn, op, init, final_reduce):
        def body(c, acc):
            return op(acc, fold(fn(score_ref[c], c), op))
        acc = lax.fori_loop(0, n_chunks, body, jnp.full(acc_shape, init, F32))
        return final_reduce(acc, axis=key_axis, keepdims=True)

    def count(pred):
        return reduce_all(lambda x, c: jnp.where(pred(x, c), 1.0, 0.0), jnp.add, 0.0, jnp.sum)

    def max_below(v):
        return reduce_all(lambda x, c: jnp.where(x < v, x, neg), jnp.maximum, neg, jnp.max)

    small = n_valid <= k
    row_min = reduce_all(lambda x, c: jnp.where(x > neg, x, pos), jnp.minimum, pos, jnp.min)

    def cond(s):
        return jnp.sum(1.0 - s[5]) > 0.0

    def finished(c_lo, c_hi, done):
        return jnp.where(jnp.logical_or(c_hi == kf - 1.0, c_lo == kf), 1.0, done)

    def body(s):
        lo, hi, c_lo, c_hi, snap, done = s
        vmax = max_below(hi)
        c = count(lambda x, cc: x >= vmax)
        live = done < 0.5
        hit = jnp.logical_and(live, c >= kf)
        move = jnp.logical_and(live, c < kf)
        snap = jnp.where(hit, 1.0, snap)
        hi = jnp.where(move, vmax, hi)
        c_hi = jnp.where(move, c, c_hi)
        done = finished(c_lo, c_hi, jnp.where(hit, 1.0, done))
        for _ in range(BISECT_STEPS):
            mid = 0.5 * lo + 0.5 * hi
            c = count(lambda x, cc: x >= mid)
            live = done < 0.5
            up = jnp.logical_and(live, c >= kf)
            down = jnp.logical_and(live, c < kf)
            lo = jnp.where(up, mid, lo)
            c_lo = jnp.where(up, c, c_lo)
            hi = jnp.where(down, mid, hi)
            c_hi = jnp.where(down, c, c_hi)
            done = finished(c_lo, c_hi, done)
        return lo, hi, c_lo, c_hi, snap, done

    col = lambda v: jnp.full(n_valid.shape, v, F32)
    init = (row_min, col(pos), n_valid.astype(F32), col(0.0), col(0.0), jnp.where(small, 1.0, 0.0).astype(F32))
    lo, hi, c_lo, c_hi, snap, _ = lax.while_loop(cond, body, init)
    from_hi = jnp.logical_or(snap > 0.5, c_hi == kf - 1.0)
    above_lo = reduce_all(lambda x, c: jnp.where(x >= lo, x, pos), jnp.minimum, pos, jnp.min)
    thr = jnp.where(small, neg, jnp.where(from_hi, max_below(hi), above_lo))

    need = kf - count(lambda x, c: x > thr)
    n_eq = count(lambda x, c: x == thr)
    tie = jnp.logical_and(jnp.logical_not(small), n_eq > need)
    total = score_ref.shape[0] * width
    col0 = lax.broadcasted_iota(jnp.int32, (1, width) if key_axis == 1 else (width, 1), key_axis)

    def tie_search():
        lo = col(-1.0)
        hi = col(float(total - 1))
        for _ in range(int(np.ceil(np.log2(total))) + 1):
            mid = jnp.floor(0.5 * (lo + hi))
            ok = count(lambda x, c: jnp.logical_and(x == thr, (col0 + c * width).astype(F32) <= mid)) >= need
            hi = jnp.where(ok, mid, hi)
            lo = jnp.where(ok, lo, mid)
        return hi

    any_tie = jnp.sum(jnp.where(tie, 1.0, 0.0)) > 0.0
    jthr = lax.cond(any_tie, tie_search, lambda: col(float(total)))
    jthr = jnp.where(small, -1.0, jthr)
    return thr, jthr


ACC_W = HEAD_DIM + BF16_ROWS


def _with_ones(vt):
    return jnp.concatenate([vt, jnp.ones((BF16_ROWS, vt.shape[1]), vt.dtype)], axis=0)


def _logits_step(c, h, s, s_ref, mx_ref):
    s_ref[c, h] = s
    mx_ref[h] = jnp.maximum(mx_ref[h], _fold_lanes(s, jnp.maximum))


def _row_max(mx_ref, m_ref):
    for h in range(N_HEADS):
        m_ref[h] = jnp.max(mx_ref[h], axis=1, keepdims=True)


def _values_step(c, h, vt_ones, s_ref, m_ref, acc_ref):
    p = jnp.exp(s_ref[c, h] - m_ref[h])
    acc_ref[h] += lax.dot_general(p.astype(BF16), vt_ones, _NT, preferred_element_type=F32)


def _softmax_finish(o_ref, acc_ref):
    outs = []
    for h in range(N_HEADS):
        acc = acc_ref[h]
        outs.append(acc[:, :HEAD_DIM] / acc[:, HEAD_DIM:HEAD_DIM + 1])
    o_ref[...] = jnp.concatenate(outs, axis=1).astype(o_ref.dtype)


def _dsa_prompt_kernel(qa_ref, qi_ref, wi_ref, kit_ref, kvt_ref, o_ref,
                       score_ref, scoret_ref, qs_ref, s_ref, mx_ref, m_ref, acc_ref, *, k_sel):
    i = pl.program_id(1)
    _, tq, kc = score_ref.shape
    n_kc = (i * tq + tq + kc - 1) // kc
    rows = i * tq + lax.broadcasted_iota(jnp.int32, (tq, 1), 0)
    w = wi_ref[:, :IDX_HEADS]
    col0 = lax.broadcasted_iota(jnp.int32, (1, kc), 1)

    def idx_body(c, carry):
        k0 = pl.multiple_of(c * kc, kc)
        kblk = kit_ref[0, :, pl.ds(k0, kc)]
        acc = jnp.zeros((tq, kc), F32)
        for h in range(IDX_HEADS):
            s = jnp.dot(qi_ref[:, h * IDX_DIM:(h + 1) * IDX_DIM], kblk, preferred_element_type=F32)
            acc = acc + jnp.maximum(s, 0.0) * w[:, h:h + 1]
        masked = jnp.where(col0 + k0 <= rows, acc, -jnp.inf)
        score_ref[c] = masked
        scoret_ref[c] = masked.T
        return carry

    lax.fori_loop(0, n_kc, idx_body, 0)
    n_valid = i * tq + 1 + lax.broadcasted_iota(jnp.int32, (1, tq), 1)
    thr_t, jthr_t = _select_threshold(scoret_ref, n_kc, n_valid, k_sel, 0)
    to_column = lambda v: jnp.broadcast_to(v, (LANES, tq)).T[:, :1]
    thr, jthr = to_column(thr_t), to_column(jthr_t)

    qs_ref[...] = qa_ref[...] * SM_SCALE
    mx_ref[...] = jnp.full(mx_ref.shape, MASKED, F32)
    acc_ref[...] = jnp.zeros(acc_ref.shape, F32)

    def logits_body(c, carry):
        k0 = pl.multiple_of(c * kc, kc)
        x = score_ref[c]
        sel = jnp.logical_or(x > thr, jnp.logical_and(x == thr, (col0 + k0).astype(F32) <= jthr))
        bias = jnp.where(sel, 0.0, MASKED)
        for g in range(N_KV):
            kt = kvt_ref[0, 2 * g, :, pl.ds(k0, kc)]
            for r in range(GROUP):
                h = g * GROUP + r
                s = jnp.dot(qs_ref[:, h * HEAD_DIM:(h + 1) * HEAD_DIM], kt, preferred_element_type=F32) + bias
                _logits_step(c, h, s, s_ref, mx_ref)
        return carry

    lax.fori_loop(0, n_kc, logits_body, 0)
    _row_max(mx_ref, m_ref)

    def values_body(c, carry):
        k0 = pl.multiple_of(c * kc, kc)
        for g in range(N_KV):
            vt = _with_ones(kvt_ref[0, 2 * g + 1, :, pl.ds(k0, kc)])
            for r in range(GROUP):
                _values_step(c, g * GROUP + r, vt, s_ref, m_ref, acc_ref)
        return carry

    lax.fori_loop(0, n_kc, values_body, 0)
    _softmax_finish(o_ref, acc_ref)


def _dsa_prompt(pp, b, t, tq, kc):
    n_t = t // tq
    k_sel = min(IDX_TOPK, t // 4)
    kern = functools.partial(_dsa_prompt_kernel, k_sel=k_sel)
    return pl.pallas_call(
        kern, grid=(b, n_t),
        in_specs=[pl.BlockSpec((tq, WIDTH), lambda bb, i: (bb * n_t + i, 0)),
                  pl.BlockSpec((tq, WIDTH), lambda bb, i: (bb * n_t + i, 1)),
                  pl.BlockSpec((tq, LANES), lambda bb, i: (bb * n_t + i, 0)),
                  pl.BlockSpec((1, IDX_DIM, t), lambda bb, i: (bb, 0, 0)),
                  pl.BlockSpec((1, KV_SLABS, HEAD_DIM, t), lambda bb, i: (bb, 0, 0, 0))],
        out_specs=pl.BlockSpec((tq, WIDTH), lambda bb, i: (bb * n_t + i, 0)),
        out_shape=jax.ShapeDtypeStruct((b * t, WIDTH), BF16),
        scratch_shapes=[pltpu.VMEM((t // kc, tq, kc), F32), pltpu.VMEM((t // kc, kc, tq), F32),
                        pltpu.VMEM((tq, WIDTH), BF16),
                        pltpu.VMEM((t // kc, N_HEADS, tq, kc), F32), pltpu.VMEM((N_HEADS, tq, LANES), F32),
                        pltpu.VMEM((N_HEADS, tq, 1), F32), pltpu.VMEM((N_HEADS, tq, ACC_W), F32)],
        compiler_params=_cparams("arbitrary", "arbitrary"), name="dsa_prompt",
    )(pp["q16"], pp["q16"], pp["wi"], pp["kit16"], pp["kvat16"])


def _top_blocks(gate, valid, axis):
    idx = lax.broadcasted_iota(jnp.int32, gate.shape, axis).astype(F32)
    g = jnp.where(valid, gate, -jnp.inf)
    sel = jnp.zeros(gate.shape, F32)
    for _ in range(MOBA_TOPK):
        m = jnp.max(g, axis=axis, keepdims=True)
        first = jnp.min(jnp.where(g == m, idx, float(gate.shape[axis])), axis=axis, keepdims=True)
        pick = idx == first
        sel = jnp.where(jnp.logical_and(pick, m > -jnp.inf), 1.0, sel)
        g = jnp.where(pick, -jnp.inf, g)
    return sel


def _block_indicator(n_rows, n_cols, scale):
    blk = lax.broadcasted_iota(jnp.int32, (n_rows, n_cols), 1) // MOBA_BLOCK
    return jnp.where(blk == lax.broadcasted_iota(jnp.int32, (n_rows, n_cols), 0), scale, 0.0).astype(BF16)


def _moba_prompt_kernel(qb_ref, kvt_ref, o_ref, kext_ref, kmean_ref, qe_ref, s_ref, mx_ref, m_ref, acc_ref):
    own = pl.program_id(1)
    t = kvt_ref.shape[3]
    tq = qb_ref.shape[0]
    n_ext = kext_ref.shape[1] - HEAD_DIM

    @pl.when(own == 0)
    def _():
        ind = _block_indicator(n_ext, t, 1.0)
        for g in range(N_KV):
            kt = kvt_ref[0, 2 * g]
            kext_ref[g, :HEAD_DIM, :] = kt
            kext_ref[g, HEAD_DIM:, :] = ind
            kmean_ref[g] = lax.dot_general(ind, kt, _NT, preferred_element_type=F32) * (1.0 / MOBA_BLOCK)

    blk = lax.broadcasted_iota(jnp.int32, (n_ext, tq), 0)
    for h in range(N_HEADS):
        g = h // GROUP
        q = qb_ref[:, h * HEAD_DIM:(h + 1) * HEAD_DIM]
        gate_t = lax.dot_general(kmean_ref[g].astype(BF16), q, _NT, preferred_element_type=F32)
        picked_t = _top_blocks(gate_t, blk < own, 0)
        allowed = jnp.logical_or(picked_t > 0.5, blk == own)
        bias = _pad_rows(jnp.where(allowed, 0.0, MASKED), LANES).T[:, :n_ext]
        qe_ref[h] = jnp.concatenate([q * SM_SCALE, bias.astype(BF16)], axis=1)

    mx_ref[...] = jnp.full(mx_ref.shape, MASKED, F32)
    acc_ref[...] = jnp.zeros(acc_ref.shape, F32)

    def logits(n, extra_bias):
        k0 = pl.multiple_of(n * MOBA_BLOCK, MOBA_BLOCK)
        for g in range(N_KV):
            ke = kext_ref[g, :, pl.ds(k0, MOBA_BLOCK)]
            for r in range(GROUP):
                h = g * GROUP + r
                s = jnp.dot(qe_ref[h], ke, preferred_element_type=F32)
                _logits_step(n, h, s if extra_bias is None else s + extra_bias, s_ref, mx_ref)

    def logits_body(n, carry):
        logits(n, None)
        return carry

    lax.fori_loop(0, own, logits_body, 0)
    causal = (lax.broadcasted_iota(jnp.int32, (tq, MOBA_BLOCK), 1)
              <= lax.broadcasted_iota(jnp.int32, (tq, MOBA_BLOCK), 0))
    logits(own, jnp.where(causal, 0.0, MASKED))
    _row_max(mx_ref, m_ref)

    def values_body(n, carry):
        k0 = pl.multiple_of(n * MOBA_BLOCK, MOBA_BLOCK)
        for g in range(N_KV):
            vt = _with_ones(kvt_ref[0, 2 * g + 1, :, pl.ds(k0, MOBA_BLOCK)])
            for r in range(GROUP):
                _values_step(n, g * GROUP + r, vt, s_ref, m_ref, acc_ref)
        return carry

    lax.fori_loop(0, own + 1, values_body, 0)
    _softmax_finish(o_ref, acc_ref)


def _moba_prompt(pp, b, t):
    tq = MOBA_BLOCK
    n_t = t // tq
    assert t % MOBA_BLOCK == 0 and n_t <= LANES
    n_ext = -(-n_t // BF16_ROWS) * BF16_ROWS
    return pl.pallas_call(
        _moba_prompt_kernel, grid=(b, n_t),
        in_specs=[pl.BlockSpec((tq, WIDTH), lambda bb, i: (bb * n_t + i, 2)),
                  pl.BlockSpec((1, KV_SLABS, HEAD_DIM, t), lambda bb, i: (bb, 0, 0, 0))],
        out_specs=pl.BlockSpec((tq, WIDTH), lambda bb, i: (bb * n_t + i, 0)),
        out_shape=jax.ShapeDtypeStruct((b * t, WIDTH), BF16),
        scratch_shapes=[pltpu.VMEM((N_KV, HEAD_DIM + n_ext, t), BF16), pltpu.VMEM((N_KV, n_ext, HEAD_DIM), F32),
                        pltpu.VMEM((N_HEADS, tq, HEAD_DIM + n_ext), BF16),
                        pltpu.VMEM((n_t, N_HEADS, tq, MOBA_BLOCK), F32), pltpu.VMEM((N_HEADS, tq, LANES), F32),
                        pltpu.VMEM((N_HEADS, tq, 1), F32), pltpu.VMEM((N_HEADS, tq, ACC_W), F32)],
        compiler_params=_cparams("arbitrary", "arbitrary"), name="moba_prompt",
    )(pp["q16"], pp["kvbt16"])


def _page_copy(pt_ref, pool, buf, sems, j, b, slot, p):
    off = pl.multiple_of(p * PAGE_SIZE, PAGE_SIZE)
    lead = (slice(None),) * (len(buf.shape) - 2)
    return pltpu.make_async_copy(pool.at[pt_ref[b, p]], buf.at[(slot,) + lead + (pl.ds(off, PAGE_SIZE),)],
                                 sems.at[2 * j + slot])


def _paged_fetch(pt_ref, pools, bufs, sems, past):
    b = pl.program_id(0)
    nb = pl.num_programs(0)
    slot = b % 2
    n_pages = past // PAGE_SIZE

    def start(bb, sl):
        def body(p, carry):
            for j, (pool, buf) in enumerate(zip(pools, bufs)):
                _page_copy(pt_ref, pool, buf, sems, j, bb, sl, p).start()
            return carry
        lax.fori_loop(0, n_pages, body, 0)

    @pl.when(b == 0)
    def _():
        start(0, 0)

    @pl.when(b + 1 < nb)
    def _():
        start(b + 1, 1 - slot)

    def wait_body(p, carry):
        for j, (pool, buf) in enumerate(zip(pools, bufs)):
            _page_copy(pt_ref, pool, buf, sems, j, b, slot, p).wait()
        return carry
    lax.fori_loop(0, n_pages, wait_body, 0)
    return slot


def _stack_heads(q, heads):
    qf = q.astype(F32)
    return jnp.concatenate([qf[:, h * HEAD_DIM:(h + 1) * HEAD_DIM] for h in heads], axis=0).astype(BF16)


def _pad_rows(x, n):
    return jnp.concatenate([x, jnp.zeros((n - x.shape[0], x.shape[1]), x.dtype)], axis=0)


def _sample_attention(q, kt_past, vt_past, k_new, v_new, bias):
    k_new = _pad_rows(k_new, LANES).astype(BF16)
    v_new = _pad_rows(v_new, LANES).astype(BF16)
    past = kt_past.shape[1]
    s_past = jnp.dot(q, kt_past.astype(BF16), preferred_element_type=F32)
    s_new = lax.dot_general(q, k_new, _NT, preferred_element_type=F32)
    s = jnp.concatenate([s_past, s_new], axis=1) * SM_SCALE + bias
    m = jnp.max(s, axis=1, keepdims=True)
    p = jnp.exp(s - m)
    l = jnp.sum(p, axis=1, keepdims=True)
    p = p.astype(BF16)
    o = (lax.dot_general(p[:, :past], vt_past.astype(BF16), _NT, preferred_element_type=F32)
         + jnp.dot(p[:, past:], v_new, preferred_element_type=F32))
    return o / l


def _store_heads(o_ref, per_group, nq):
    outs = [per_group[g][r * nq:(r + 1) * nq, :] for g in range(N_KV) for r in range(GROUP)]
    o_ref[0] = jnp.concatenate(outs, axis=1).astype(o_ref.dtype)


def _dsa_decode_kernel(pt_ref, q_ref, kiw_ref, kva_ref, idx_pool, kv_pool, o_ref,
                       idxbuf, kvbuf, score_ref, sems, *, past, k_sel):
    nq = q_ref.shape[1]
    lp = past + LANES
    slot = _paged_fetch(pt_ref, (idx_pool, kv_pool), (idxbuf, kvbuf), sems, past)
    kiw = kiw_ref[0]
    kv_new = kva_ref[0]
    q = q_ref[0]
    qa, qi = q[:, :WIDTH], q[:, WIDTH:2 * WIDTH]

    qi_rows = _stack_heads(qi, range(IDX_HEADS))
    s = jnp.concatenate(
        [jnp.dot(qi_rows, idxbuf[slot].astype(BF16), preferred_element_type=F32),
         lax.dot_general(qi_rows, _pad_rows(kiw[:, :IDX_DIM], LANES).astype(BF16), _NT, preferred_element_type=F32)],
        axis=1)
    score = jnp.zeros((nq, lp), F32)
    for h in range(IDX_HEADS):
        score = score + jnp.maximum(s[h * nq:(h + 1) * nq, :], 0.0) * kiw[:, IDX_DIM + h:IDX_DIM + h + 1]
    col = lax.broadcasted_iota(jnp.int32, (1, lp), 1)
    qpos = past + lax.broadcasted_iota(jnp.int32, (nq, 1), 0)
    score_ref[0] = jnp.where(col <= qpos, score, -jnp.inf)
    thr, jthr = _select_threshold(score_ref, 1, qpos + 1, k_sel, 1)
    x = score_ref[0]
    sel = jnp.logical_or(x > thr, jnp.logical_and(x == thr, col.astype(F32) <= jthr))
    bias = jnp.where(sel, 0.0, MASKED)
    bias = jnp.concatenate([bias] * GROUP, axis=0)

    outs = []
    for g in range(N_KV):
        qg = _stack_heads(qa, range(g * GROUP, (g + 1) * GROUP))
        k_new = kv_new[:, 2 * g * HEAD_DIM:(2 * g + 1) * HEAD_DIM]
        v_new = kv_new[:, (2 * g + 1) * HEAD_DIM:(2 * g + 2) * HEAD_DIM]
        outs.append(_sample_attention(qg, kvbuf[slot, 2 * g], kvbuf[slot, 2 * g + 1], k_new, v_new, bias))
    _store_heads(o_ref, outs, nq)


def _moba_decode_kernel(pt_ref, q_ref, kvb_ref, kv_pool, o_ref, kvbuf, sems, *, past):
    nq = q_ref.shape[1]
    n_blk = past // MOBA_BLOCK
    slot = _paged_fetch(pt_ref, (kv_pool,), (kvbuf,), sems, past)
    kv_new = kvb_ref[0]
    qb = q_ref[0][:, 2 * WIDTH:3 * WIDTH]

    rows = GROUP * nq
    col = lax.broadcasted_iota(jnp.int32, (rows, LANES), 1)
    qidx = lax.broadcasted_iota(jnp.int32, (rows, LANES), 0) % nq
    bias_new = jnp.where(col <= qidx, 0.0, MASKED)
    ind = _block_indicator(n_blk, past, 1.0)

    outs = []
    for g in range(N_KV):
        kt = kvbuf[slot, 2 * g]
        kmean = lax.dot_general(ind, kt.astype(BF16), _NT, preferred_element_type=F32) * (1.0 / MOBA_BLOCK)
        qg = _stack_heads(qb, range(g * GROUP, (g + 1) * GROUP))
        gate = lax.dot_general(qg, kmean.astype(BF16), _NT, preferred_element_type=F32)
        picked = _top_blocks(gate, jnp.full(gate.shape, True), 1)
        bias_blk = jnp.where(picked > 0.5, 0.0, MASKED).astype(BF16)
        bias_past = jnp.dot(bias_blk, ind, preferred_element_type=F32)
        bias = jnp.concatenate([bias_past, bias_new], axis=1)
        k_new = kv_new[:, 2 * g * HEAD_DIM:(2 * g + 1) * HEAD_DIM]
        v_new = kv_new[:, (2 * g + 1) * HEAD_DIM:(2 * g + 2) * HEAD_DIM]
        outs.append(_sample_attention(qg, kt, kvbuf[slot, 2 * g + 1], k_new, v_new, bias))
    _store_heads(o_ref, outs, nq)


def _sample_specs(nq, widths):
    return [pl.BlockSpec((1, nq, w), lambda bb, pt: (bb, 0, 0)) for w in widths]


def _dsa_decode(ps, page_table, idx_pool_t, kv_pool_t, db, nq):
    past = page_table.shape[1] * PAGE_SIZE
    lp = past + LANES
    k_sel = min(IDX_TOPK, (past + nq) // 4)
    qw = ps["q16"].shape[1]
    kern = functools.partial(_dsa_decode_kernel, past=past, k_sel=k_sel)
    grid_spec = pltpu.PrefetchScalarGridSpec(
        num_scalar_prefetch=1, grid=(db,),
        in_specs=_sample_specs(nq, [qw, LANES, KV_WIDTH])
        + [pl.BlockSpec(memory_space=pl.ANY), pl.BlockSpec(memory_space=pl.ANY)],
        out_specs=pl.BlockSpec((1, nq, WIDTH), lambda bb, pt: (bb, 0, 0)),
        scratch_shapes=[pltpu.VMEM((2, IDX_DIM, past), F32), pltpu.VMEM((2, KV_SLABS, HEAD_DIM, past), F32),
                        pltpu.VMEM((1, nq, lp), F32), pltpu.SemaphoreType.DMA((4,))])
    return pl.pallas_call(
        kern, grid_spec=grid_spec, out_shape=jax.ShapeDtypeStruct((db, nq, WIDTH), BF16),
        compiler_params=_cparams("arbitrary"), name="dsa_decode",
    )(page_table, ps["q16"].reshape(db, nq, qw), ps["kiw"].reshape(db, nq, LANES),
      ps["kva"].reshape(db, nq, KV_WIDTH), idx_pool_t, kv_pool_t)


def _moba_decode(ps, page_table, kv_pool_t, db, nq):
    past = page_table.shape[1] * PAGE_SIZE
    assert past % MOBA_BLOCK == 0 and nq <= MOBA_BLOCK
    qw = ps["q16"].shape[1]
    kern = functools.partial(_moba_decode_kernel, past=past)
    grid_spec = pltpu.PrefetchScalarGridSpec(
        num_scalar_prefetch=1, grid=(db,),
        in_specs=_sample_specs(nq, [qw, KV_WIDTH]) + [pl.BlockSpec(memory_space=pl.ANY)],
        out_specs=pl.BlockSpec((1, nq, WIDTH), lambda bb, pt: (bb, 0, 0)),
        scratch_shapes=[pltpu.VMEM((2, KV_SLABS, HEAD_DIM, past), F32), pltpu.SemaphoreType.DMA((2,))])
    return pl.pallas_call(
        kern, grid_spec=grid_spec, out_shape=jax.ShapeDtypeStruct((db, nq, WIDTH), BF16),
        compiler_params=_cparams("arbitrary"), name="moba_decode",
    )(page_table, ps["q16"].reshape(db, nq, qw), ps["kvb"].reshape(db, nq, KV_WIDTH), kv_pool_t)


def _out_kernel(x_ref, oa_ref, ob_ref, gate_ref, wba_ref, wbb_ref, wo_ref, y_ref):
    d = x_ref.shape[1]
    ua = oa_ref[...] * gate_ref[:, 0:WIDTH]
    ub = ob_ref[...] * gate_ref[:, WIDTH:2 * WIDTH]
    a = jnp.dot(ua, wba_ref[...], preferred_element_type=F32)
    b = jnp.dot(ub, wbb_ref[...], preferred_element_type=F32)
    ga = gate_ref[:, 2 * WIDTH:2 * WIDTH + d].astype(F32)
    gb = gate_ref[:, 2 * WIDTH + d:2 * WIDTH + 2 * d].astype(F32)
    merged = (ga * a + gb * b).astype(BF16)
    y_ref[...] = x_ref[...] + jnp.dot(merged, wo_ref[...], preferred_element_type=F32)


def _out_proj(x2d, oa, ob, gate16, wts, tm):
    n, d = x2d.shape
    row = lambda w: pl.BlockSpec((tm, w), lambda i: (i, 0))
    return pl.pallas_call(
        _out_kernel, grid=(n // tm,),
        in_specs=[row(d), row(WIDTH), row(WIDTH), row(gate16.shape[1]),
                  _const_spec((WIDTH, d)), _const_spec((WIDTH, d)), _const_spec((d, d))],
        out_specs=row(d), out_shape=jax.ShapeDtypeStruct((n, d), F32),
        compiler_params=_cparams("arbitrary"), name="out_proj",
    )(x2d, oa, ob, gate16, wts["wba"], wts["wbb"], wts["wo"])


def _rope_angles(pos):
    inv = ROPE_THETA ** (-jnp.arange(ROT_HALF, dtype=F32) / ROT_HALF)
    ang = pos.astype(F32)[:, None] * inv[None, :]
    return jnp.cos(ang), jnp.sin(ang)


def _rope_tables(pos):
    p = pos.shape[0]
    c, s = _rope_angles(pos)
    rest = HEAD_DIM - 2 * ROT_HALF
    z8, zr = jnp.zeros((p, ROT_HALF), F32), jnp.zeros((p, rest), F32)
    c_head = jnp.concatenate([c, c, jnp.ones((p, rest), F32)], axis=1)
    s1_head = jnp.concatenate([-s, z8, zr], axis=1)
    s2_head = jnp.concatenate([z8, s, zr], axis=1)
    one, zero = jnp.ones((p, HEAD_DIM), F32), jnp.zeros((p, HEAD_DIM), F32)
    cat = lambda a, b: jnp.concatenate([a, b], axis=1)
    return (cat(c_head, c_head), cat(s1_head, s1_head), cat(s2_head, s2_head),
            cat(c_head, one), cat(s1_head, zero), cat(s2_head, zero))


def _prep_weights(norm_g, w_in, qn_a, kn_a, qn_i, kn_i, qn_b, kn_b, w_ba, w_bb, w_out):
    d = w_in.shape[0]
    kvw = N_KV * HEAD_DIM
    splits = (WIDTH, kvw, kvw, IDX_HEADS * IDX_DIM, IDX_DIM, IDX_HEADS, WIDTH, WIDTH, kvw, kvw, WIDTH, d, d)
    offs = np.concatenate([[0], np.cumsum(splits)])
    qa, ka, va, qi, ki, wi, za, qb, kb, vb, zb, ga, gb = [w_in[:, offs[j]:offs[j + 1]] for j in range(13)]
    hd = HEAD_DIM
    pad = jnp.zeros((d, LANES - IDX_DIM - IDX_HEADS), F32)
    slabs = jnp.concatenate([ka[:, :hd], va[:, :hd], ka[:, hd:], va[:, hd:],
                             kb[:, :hd], vb[:, :hd], kb[:, hd:], vb[:, hd:], ki], axis=1)
    wk = jnp.concatenate([slabs, wi, pad], axis=1)
    wwi = jnp.concatenate([wi, jnp.zeros((d, LANES - IDX_HEADS), F32)], axis=1)
    zero = jnp.zeros((hd,), F32)
    one = jnp.ones((hd,), F32)
    wi_gain = jnp.concatenate([jnp.full((IDX_HEADS,), (IDX_HEADS * IDX_DIM) ** -0.5, F32),
                               jnp.zeros((hd - IDX_HEADS,), F32)])
    gka = jnp.concatenate([kn_a, zero, kn_a, zero, kn_b, zero, kn_b, zero, kn_i, zero])[None, :]
    gkb = jnp.concatenate([zero, one, zero, one, zero, one, zero, one, zero, wi_gain])[None, :]
    gkt = jnp.concatenate([kn_a, one, kn_a, one, kn_b, one, kn_b, one, kn_i])[:, None]
    gq = jnp.concatenate([jnp.tile(qn_a, N_HEADS), jnp.tile(qn_i, IDX_HEADS), jnp.tile(qn_b, N_HEADS)])[None, :]
    blk = np.arange(MXU_DIM) // HEAD_DIM
    bd = jnp.asarray((blk[:, None] == blk[None, :]) / HEAD_DIM, BF16)
    return dict(
        ng=norm_g[None, :].astype(F32),
        wq=jnp.concatenate([qa, qi, qb], axis=1).astype(BF16), wk=wk.astype(BF16),
        wkt=slabs.T.astype(BF16), wwi=wwi.astype(BF16),
        wg=jnp.concatenate([za, zb, ga, gb], axis=1).astype(BF16),
        gq=gq, gka=gka, gkb=gkb, gkt=gkt, bd=bd,
        wba=w_ba.astype(BF16), wbb=w_bb.astype(BF16), wo=w_out.astype(BF16))


def kernel(x_prompt, x_sample, cache_kv_a, cache_idx_k, cache_kv_b, page_table, norm_g, w_in, q_norm_a, k_norm_a,
           idx_q_norm, idx_k_norm, q_norm_b, k_norm_b, w_branch_a, w_branch_b, w_out):
    b, t, d = x_prompt.shape
    db, nq, _ = x_sample.shape
    depth = w_in.shape[0]
    n_phys = cache_kv_a.shape[1]
    past = page_table.shape[1] * PAGE_SIZE
    tm = 256
    tm_s = min(tm, db * nq)
    assert t % tm == 0 and tm_s % nq == 0 and (db * nq) % tm_s == 0

    pos_p = jnp.arange(t, dtype=jnp.int32)
    cos_p, sin_p = _rope_angles(pos_p)
    tab_p = _rope_tables(pos_p)[:3] + (cos_p.T, sin_p.T)
    tab_s = tuple(jnp.tile(a, (tm_s // nq, 1)) for a in _rope_tables(past + jnp.arange(nq, dtype=jnp.int32)))

    kv_t = lambda pool: jnp.transpose(pool, (0, 2, 3, 4, 1)).reshape(n_phys, KV_SLABS, HEAD_DIM, PAGE_SIZE)

    hp = x_prompt.reshape(b * t, d)
    hs = x_sample.reshape(db * nq, d)
    new = [[] for _ in range(6)]
    for l in range(depth):
        wts = _prep_weights(norm_g[l], w_in[l], q_norm_a[l], k_norm_a[l], idx_q_norm[l], idx_k_norm[l],
                            q_norm_b[l], k_norm_b[l], w_branch_a[l], w_branch_b[l], w_out[l])
        pp = _project_prompt(hp, tab_p, wts, b, t, tm)
        oa = _dsa_prompt(pp, b, t, tq=256, kc=min(512, t))
        ob = _moba_prompt(pp, b, t)
        hp = _out_proj(hp, oa, ob, pp["gate16"], wts, tm)

        ps = _project_sample(hs, tab_s, wts, tm_s)
        oa_s = _dsa_decode(ps, page_table, jnp.transpose(cache_idx_k[l], (0, 2, 1)), kv_t(cache_kv_a[l]), db, nq)
        ob_s = _moba_decode(ps, page_table, kv_t(cache_kv_b[l]), db, nq)
        hs = _out_proj(hs, oa_s.reshape(db * nq, WIDTH), ob_s.reshape(db * nq, WIDTH), ps["gate16"], wts, tm_s)

        to_tokens = lambda a: jnp.transpose(a.reshape(b, N_KV, 2, HEAD_DIM, t), (0, 4, 1, 2, 3))
        new[0].append(to_tokens(pp["kvat"]))
        new[1].append(jnp.transpose(pp["kit"], (0, 2, 1)))
        new[2].append(to_tokens(pp["kvbt"]))
        new[3].append(ps["kva"].reshape(db, nq, N_KV, 2, HEAD_DIM))
        new[4].append(ps["kiw"][:, :IDX_DIM].reshape(db, nq, IDX_DIM))
        new[5].append(ps["kvb"].reshape(db, nq, N_KV, 2, HEAD_DIM))
    return (hp.reshape(b, t, d), hs.reshape(db, nq, d)) + tuple(jnp.stack(a, axis=0) for a in new)
```

```python
import functools
import math

import numpy as np
import jax
import jax.numpy as jnp
from jax import lax
from jax.experimental import pallas as pl
from jax.experimental.pallas import tpu as pltpu

F32 = jnp.float32
BF16 = jnp.bfloat16

HEAD_DIM = 64
N_HEADS = 8
N_KV = 2
GROUP = N_HEADS // N_KV
WIDTH = N_HEADS * HEAD_DIM
KV_WIDTH = N_KV * 2 * HEAD_DIM
KV_SLABS = N_KV * 2
IDX_HEADS = 8
IDX_DIM = 64
IDX_TOPK = 256
MOBA_BLOCK = 256
MOBA_TOPK = 3
PAGE_SIZE = 128
ROPE_THETA = 500000.0
ROT_HALF = HEAD_DIM // 4 // 2
RMS_EPS = 1e-6
SM_SCALE = HEAD_DIM ** -0.5

LANES = 128
SUBLANES = 8
MXU_DIM = 256
BF16_ROWS = 16
VMEM_LIMIT_BYTES = 56 * 1024 * 1024

MASKED = -1e30
BISECT_STEPS = 6

_NT = (((1,), (1,)), ((), ()))


def _cparams(*sem):
    return pltpu.CompilerParams(dimension_semantics=sem, vmem_limit_bytes=VMEM_LIMIT_BYTES)


def _normed_input(x_ref, ng_ref):
    x = x_ref[...]
    inv = lax.rsqrt(jnp.mean(x * x, axis=-1, keepdims=True) + RMS_EPS)
    return (x * inv * ng_ref[...]).astype(BF16)


def _head_mean_sq(z, bd_ref):
    parts = []
    for c in range(0, z.shape[1], MXU_DIM):
        cw = min(MXU_DIM, z.shape[1] - c)
        zc = z[:, c:c + cw]
        parts.append(jnp.dot((zc * zc).astype(BF16), bd_ref[:cw, :cw], preferred_element_type=F32))
    return parts[0] if len(parts) == 1 else jnp.concatenate(parts, axis=1)


def _rope_lanes(y, c_ref, s1_ref, s2_ref):
    return (y * c_ref[...] + pltpu.roll(y, LANES - ROT_HALF, 1) * s1_ref[...]
            + pltpu.roll(y, ROT_HALF, 1) * s2_ref[...])


def _queries_and_gates(h, wq_ref, wg_ref, gq_ref, bd_ref, cq_ref, s1q_ref, s2q_ref, q16_ref, gate16_ref):
    for c0 in range(0, q16_ref.shape[1], 512):
        z = jnp.dot(h, wq_ref[:, c0:c0 + 512], preferred_element_type=F32)
        y = z * lax.rsqrt(_head_mean_sq(z, bd_ref) + RMS_EPS) * gq_ref[:, c0:c0 + 512]
        for j in range(0, 512, LANES):
            q16_ref[:, c0 + j:c0 + j + LANES] = _rope_lanes(y[:, j:j + LANES], cq_ref, s1q_ref, s2q_ref).astype(BF16)
    n_silu = 2 * WIDTH
    for c0 in range(0, gate16_ref.shape[1], 512):
        z = jnp.dot(h, wg_ref[:, c0:c0 + 512], preferred_element_type=F32)
        sig = 1.0 / (1.0 + jnp.exp(-z))
        gate16_ref[:, c0:c0 + 512] = (z * sig if c0 < n_silu else sig).astype(BF16)


def _proj_sample_kernel(x_ref, ng_ref, wq_ref, wk_ref, wg_ref, gq_ref, gka_ref, gkb_ref, bd_ref,
                        cq_ref, s1q_ref, s2q_ref, ck_ref, s1k_ref, s2k_ref,
                        q16_ref, gate16_ref, kva_ref, kvb_ref, kiw_ref):
    h = _normed_input(x_ref, ng_ref)
    _queries_and_gates(h, wq_ref, wg_ref, gq_ref, bd_ref, cq_ref, s1q_ref, s2q_ref, q16_ref, gate16_ref)
    z = jnp.dot(h, wk_ref[...], preferred_element_type=F32)
    y = z * (lax.rsqrt(_head_mean_sq(z, bd_ref) + RMS_EPS) * gka_ref[...] + gkb_ref[...])
    r = [_rope_lanes(y[:, j * LANES:(j + 1) * LANES], ck_ref, s1k_ref, s2k_ref) for j in range(5)]
    for j in range(2):
        kva_ref[:, j * LANES:(j + 1) * LANES] = r[j]
        kvb_ref[:, j * LANES:(j + 1) * LANES] = r[2 + j]
    kiw_ref[...] = r[4]


def _proj_prompt_kernel(x_ref, ng_ref, wq_ref, wkt_ref, wwi_ref, wg_ref, gq_ref, gkt_ref, bd_ref,
                        cq_ref, s1q_ref, s2q_ref, ct_ref, st_ref,
                        q16_ref, gate16_ref, wi_ref, kvat_ref, kvbt_ref, kit_ref, kvat16_ref, kvbt16_ref, kit16_ref):
    h = _normed_input(x_ref, ng_ref)
    _queries_and_gates(h, wq_ref, wg_ref, gq_ref, bd_ref, cq_ref, s1q_ref, s2q_ref, q16_ref, gate16_ref)
    wi_ref[...] = jnp.dot(h, wwi_ref[...], preferred_element_type=F32) * (IDX_HEADS * IDX_DIM) ** -0.5

    zt = lax.dot_general(wkt_ref[...], h, _NT, preferred_element_type=F32)
    cos, sin = ct_ref[...], st_ref[...]

    def key_head(j):
        z = zt[j * HEAD_DIM:(j + 1) * HEAD_DIM, :]
        y = z * lax.rsqrt(jnp.mean(z * z, axis=0, keepdims=True) + RMS_EPS) * gkt_ref[j * HEAD_DIM:(j + 1) * HEAD_DIM, :]
        y1, y2 = y[:ROT_HALF], y[ROT_HALF:2 * ROT_HALF]
        return jnp.concatenate([y1 * cos - y2 * sin, y2 * cos + y1 * sin, y[2 * ROT_HALF:]], axis=0)

    for j in range(KV_SLABS):
        for base, out_ref, out16_ref in ((0, kvat_ref, kvat16_ref), (KV_SLABS, kvbt_ref, kvbt16_ref)):
            slab = key_head(base + j) if j % 2 == 0 else zt[(base + j) * HEAD_DIM:(base + j + 1) * HEAD_DIM, :]
            out_ref[0, j] = slab
            out16_ref[0, j] = slab.astype(BF16)
    ki = key_head(2 * KV_SLABS)
    kit_ref[0] = ki
    kit16_ref[0] = ki.astype(BF16)


def _const_spec(shape, n_grid=1):
    if n_grid == 1:
        return pl.BlockSpec(shape, lambda i: (0,) * len(shape))
    return pl.BlockSpec(shape, lambda b, i: (0,) * len(shape))


def _project_sample(x2d, tables, wts, tm):
    n, d = x2d.shape
    wq, wk, wg = wts["wq"], wts["wk"], wts["wg"]
    row = lambda w: pl.BlockSpec((tm, w), lambda i: (i, 0))
    tab = pl.BlockSpec((tm, LANES), lambda i: (0, 0))
    in_specs = [row(d), _const_spec((1, d)), _const_spec(wq.shape), _const_spec(wk.shape), _const_spec(wg.shape),
                _const_spec((1, wq.shape[1])), _const_spec((1, wk.shape[1])), _const_spec((1, wk.shape[1])),
                _const_spec((MXU_DIM, MXU_DIM))] + [tab] * 6
    out_shape = [
        jax.ShapeDtypeStruct((n, wq.shape[1]), BF16),
        jax.ShapeDtypeStruct((n, wg.shape[1]), BF16),
        jax.ShapeDtypeStruct((n, KV_WIDTH), F32),
        jax.ShapeDtypeStruct((n, KV_WIDTH), F32),
        jax.ShapeDtypeStruct((n, LANES), F32),
    ]
    outs = pl.pallas_call(
        _proj_sample_kernel, grid=(n // tm,), in_specs=in_specs, out_specs=[row(s.shape[1]) for s in out_shape],
        out_shape=out_shape, compiler_params=_cparams("arbitrary"), name="project_sample",
    )(x2d, wts["ng"], wq, wk, wg, wts["gq"], wts["gka"], wts["gkb"], wts["bd"], *tables)
    return dict(zip(("q16", "gate16", "kva", "kvb", "kiw"), outs))


def _project_prompt(x2d, tables, wts, b, t, tm):
    n, d = x2d.shape
    n_t = t // tm
    wq, wkt, wwi, wg = wts["wq"], wts["wkt"], wts["wwi"], wts["wg"]
    row = lambda w: pl.BlockSpec((tm, w), lambda bb, i: (bb * n_t + i, 0))
    tab = pl.BlockSpec((tm, LANES), lambda bb, i: (i, 0))
    tab_t = pl.BlockSpec((ROT_HALF, tm), lambda bb, i: (0, i))
    cs = lambda shape: _const_spec(shape, 2)
    in_specs = [row(d), cs((1, d)), cs(wq.shape), cs(wkt.shape), cs(wwi.shape), cs(wg.shape),
                cs((1, wq.shape[1])), cs((wkt.shape[0], 1)), cs((MXU_DIM, MXU_DIM)), tab, tab, tab, tab_t, tab_t]
    kv_t = lambda dt: jax.ShapeDtypeStruct((b, KV_SLABS, HEAD_DIM, t), dt)
    ki_t = lambda dt: jax.ShapeDtypeStruct((b, IDX_DIM, t), dt)
    out_shape = [jax.ShapeDtypeStruct((n, wq.shape[1]), BF16), jax.ShapeDtypeStruct((n, wg.shape[1]), BF16),
                 jax.ShapeDtypeStruct((n, LANES), F32),
                 kv_t(F32), kv_t(F32), ki_t(F32), kv_t(BF16), kv_t(BF16), ki_t(BF16)]
    kv_spec = pl.BlockSpec((1, KV_SLABS, HEAD_DIM, tm), lambda bb, i: (bb, 0, 0, i))
    ki_spec = pl.BlockSpec((1, IDX_DIM, tm), lambda bb, i: (bb, 0, i))
    out_specs = [row(wq.shape[1]), row(wg.shape[1]), row(LANES), kv_spec, kv_spec, ki_spec, kv_spec, kv_spec, ki_spec]
    outs = pl.pallas_call(
        _proj_prompt_kernel, grid=(b, n_t), in_specs=in_specs, out_specs=out_specs, out_shape=out_shape,
        compiler_params=_cparams("arbitrary", "arbitrary"), name="project_prompt",
    )(x2d, wts["ng"], wq, wkt, wwi, wg, wts["gq"], wts["gkt"], wts["bd"], *tables)
    return dict(zip(("q16", "gate16", "wi", "kvat", "kvbt", "kit", "kvat16", "kvbt16", "kit16"), outs))


def _fold_lanes(x, op):
    acc = x[:, :LANES]
    for j in range(1, x.shape[1] // LANES):
        acc = op(acc, x[:, j * LANES:(j + 1) * LANES])
    return acc


def _fold_sublanes(x, op):
    n = x.shape[0] // SUBLANES
    parts = [x[j * SUBLANES:(j + 1) * SUBLANES, :] for j in range(min(n, SUBLANES))]
    for j in range(len(parts), n):
        parts[j % SUBLANES] = op(parts[j % SUBLANES], x[j * SUBLANES:(j + 1) * SUBLANES, :])
    while len(parts) > 1:
        parts = [op(parts[j], parts[j + 1]) for j in range(0, len(parts) - 1, 2)] + parts[len(parts) & ~1:]
    return parts[0]


def _select_threshold(score_ref, n_chunks, n_valid, k, key_axis):
    width = score_ref.shape[1 + key_axis]
    rows = score_ref.shape[2 - key_axis]
    kf = float(k)
    neg, pos = -jnp.inf, jnp.inf
    fold = _fold_lanes if key_axis == 1 else _fold_sublanes
    acc_shape = (rows, LANES) if key_axis == 1 else (SUBLANES, rows)

    def reduce_all(fn, op, init, final_reduce):
        def body(c, acc):
            return op(acc, fold(fn(score_ref[c], c), op))
        acc = lax.fori_loop(0, n_chunks, body, jnp.full(acc_shape, init, F32))
        return final_reduce(acc, axis=key_axis, keepdims=True)

    def count(pred):
        return reduce_all(lambda x, c: jnp.where(pred(x, c), 1.0, 0.0), jnp.add, 0.0, jnp.sum)

    def max_below(v):
        return reduce_all(lambda x, c: jnp.where(x < v, x, neg), jnp.maximum, neg, jnp.max)

    small = n_valid <= k
    row_min = reduce_all(lambda x, c: jnp.where(x > neg, x, pos), jnp.minimum, pos, jnp.min)

    def cond(s):
        return jnp.sum(1.0 - s[5]) > 0.0

    def finished(c_lo, c_hi, done):
        return jnp.where(jnp.logical_or(c_hi == kf - 1.0, c_lo == kf), 1.0, done)

    def body(s):
        lo, hi, c_lo, c_hi, snap, done = s
        vmax = max_below(hi)
        c = count(lambda x, cc: x >= vmax)
        live = done < 0.5
        hit = jnp.logical_and(live, c >= kf)
        move = jnp.logical_and(live, c < kf)
        snap = jnp.where(hit, 1.0, snap)
        hi = jnp.where(move, vmax, hi)
        c_hi = jnp.where(move, c, c_hi)
        done = finished(c_lo, c_hi, jnp.where(hit, 1.0, done))
        for _ in range(BISECT_STEPS):
            mid = 0.5 * lo + 0.5 * hi
            c = count(lambda x, cc: x >= mid)
            live = done < 0.5
            up = jnp.logical_and(live, c >= kf)
            down = jnp.logical_and(live, c < kf)
            lo = jnp.where(up, mid, lo)
            c_lo = jnp.where(up, c, c_lo)
            hi = jnp.where(down, mid, hi)
            c_hi = jnp.where(down, c, c_hi)
            done = finished(c_lo, c_hi, done)
        return lo, hi, c_lo, c_hi, snap, done

    col = lambda v: jnp.full(n_valid.shape, v, F32)
    init = (row_min, col(pos), n_valid.astype(F32), col(0.0), col(0.0), jnp.where(small, 1.0, 0.0).astype(F32))
    lo, hi, c_lo, c_hi, snap, _ = lax.while_loop(cond, body, init)
    from_hi = jnp.logical_or(snap > 0.5, c_hi == kf - 1.0)
    above_lo = reduce_all(lambda x, c: jnp.where(x >= lo, x, pos), jnp.minimum, pos, jnp.min)
    thr = jnp.where(small, neg, jnp.where(from_hi, max_below(hi), above_lo))

    need = kf - count(lambda x, c: x > thr)
    n_eq = count(lambda x, c: x == thr)
    tie = jnp.logical_and(jnp.logical_not(small), n_eq > need)
    total = score_ref.shape[0] * width
    col0 = lax.broadcasted_iota(jnp.int32, (1, width) if key_axis == 1 else (width, 1), key_axis)

    def tie_search():
        lo = col(-1.0)
        hi = col(float(total - 1))
        for _ in range(int(np.ceil(np.log2(total))) + 1):
            mid = jnp.floor(0.5 * (lo + hi))
            ok = count(lambda x, c: jnp.logical_and(x == thr, (col0 + c * width).astype(F32) <= mid)) >= need
            hi = jnp.where(ok, mid, hi)
            lo = jnp.where(ok, lo, mid)
        return hi

    any_tie = jnp.sum(jnp.where(tie, 1.0, 0.0)) > 0.0
    jthr = lax.cond(any_tie, tie_search, lambda: col(float(total)))
    jthr = jnp.where(small, -1.0, jthr)
    return thr, jthr


ACC_W = HEAD_DIM + BF16_ROWS


def _with_ones(vt):
    return jnp.concatenate([vt, jnp.ones((BF16_ROWS, vt.shape[1]), vt.dtype)], axis=0)


def _logits_step(c, h, s, s_ref, mx_ref):
    s_ref[c, h] = s
    mx_ref[h] = jnp.maximum(mx_ref[h], _fold_lanes(s, jnp.maximum))


def _row_max(mx_ref, m_ref):
    for h in range(N_HEADS):
        m_ref[h] = jnp.max(mx_ref[h], axis=1, keepdims=True)


def _values_step(c, h, vt_ones, s_ref, m_ref, acc_ref):
    p = jnp.exp(s_ref[c, h] - m_ref[h])
    acc_ref[h] += lax.dot_general(p.astype(BF16), vt_ones, _NT, preferred_element_type=F32)


def _softmax_finish(o_ref, acc_ref):
    outs = []
    for h in range(N_HEADS):
        acc = acc_ref[h]
        outs.append(acc[:, :HEAD_DIM] / acc[:, HEAD_DIM:HEAD_DIM + 1])
    o_ref[...] = jnp.concatenate(outs, axis=1).astype(o_ref.dtype)


def _dsa_prompt_kernel(qa_ref, qi_ref, wi_ref, kit_ref, kvt_ref, o_ref,
                       score_ref, scoret_ref, qs_ref, s_ref, mx_ref, m_ref, acc_ref, *, k_sel):
    i = pl.program_id(1)
    _, tq, kc = score_ref.shape
    n_kc = (i * tq + tq + kc - 1) // kc
    rows = i * tq + lax.broadcasted_iota(jnp.int32, (tq, 1), 0)
    w = wi_ref[:, :IDX_HEADS]
    col0 = lax.broadcasted_iota(jnp.int32, (1, kc), 1)

    def idx_body(c, carry):
        k0 = pl.multiple_of(c * kc, kc)
        kblk = kit_ref[0, :, pl.ds(k0, kc)]
        acc = jnp.zeros((tq, kc), F32)
        for h in range(IDX_HEADS):
            s = jnp.dot(qi_ref[:, h * IDX_DIM:(h + 1) * IDX_DIM], kblk, preferred_element_type=F32)
            acc = acc + jnp.maximum(s, 0.0) * w[:, h:h + 1]
        masked = jnp.where(col0 + k0 <= rows, acc, -jnp.inf)
        score_ref[c] = masked
        scoret_ref[c] = masked.T
        return carry

    lax.fori_loop(0, n_kc, idx_body, 0)
    n_valid = i * tq + 1 + lax.broadcasted_iota(jnp.int32, (1, tq), 1)
    thr_t, jthr_t = _select_threshold(scoret_ref, n_kc, n_valid, k_sel, 0)
    to_column = lambda v: jnp.broadcast_to(v, (LANES, tq)).T[:, :1]
    thr, jthr = to_column(thr_t), to_column(jthr_t)

    qs_ref[...] = qa_ref[...] * SM_SCALE
    mx_ref[...] = jnp.full(mx_ref.shape, MASKED, F32)
    acc_ref[...] = jnp.zeros(acc_ref.shape, F32)

    def logits_body(c, carry):
        k0 = pl.multiple_of(c * kc, kc)
        x = score_ref[c]
        sel = jnp.logical_or(x > thr, jnp.logical_and(x == thr, (col0 + k0).astype(F32) <= jthr))
        bias = jnp.where(sel, 0.0, MASKED)
        for g in range(N_KV):
            kt = kvt_ref[0, 2 * g, :, pl.ds(k0, kc)]
            for r in range(GROUP):
                h = g * GROUP + r
                s = jnp.dot(qs_ref[:, h * HEAD_DIM:(h + 1) * HEAD_DIM], kt, preferred_element_type=F32) + bias
                _logits_step(c, h, s, s_ref, mx_ref)
        return carry

    lax.fori_loop(0, n_kc, logits_body, 0)
    _row_max(mx_ref, m_ref)

    def values_body(c, carry):
        k0 = pl.multiple_of(c * kc, kc)
        for g in range(N_KV):
            vt = _with_ones(kvt_ref[0, 2 * g + 1, :, pl.ds(k0, kc)])
            for r in range(GROUP):
                _values_step(c, g * GROUP + r, vt, s_ref, m_ref, acc_ref)
        return carry

    lax.fori_loop(0, n_kc, values_body, 0)
    _softmax_finish(o_ref, acc_ref)


def _dsa_prompt(pp, b, t, tq, kc):
    n_t = t // tq
    k_sel = min(IDX_TOPK, t // 4)
    kern = functools.partial(_dsa_prompt_kernel, k_sel=k_sel)
    return pl.pallas_call(
        kern, grid=(b, n_t),
        in_specs=[pl.BlockSpec((tq, WIDTH), lambda bb, i: (bb * n_t + i, 0)),
                  pl.BlockSpec((tq, WIDTH), lambda bb, i: (bb * n_t + i, 1)),
                  pl.BlockSpec((tq, LANES), lambda bb, i: (bb * n_t + i, 0)),
                  pl.BlockSpec((1, IDX_DIM, t), lambda bb, i: (bb, 0, 0)),
                  pl.BlockSpec((1, KV_SLABS, HEAD_DIM, t), lambda bb, i: (bb, 0, 0, 0))],
        out_specs=pl.BlockSpec((tq, WIDTH), lambda bb, i: (bb * n_t + i, 0)),
        out_shape=jax.ShapeDtypeStruct((b * t, WIDTH), BF16),
        scratch_shapes=[pltpu.VMEM((t // kc, tq, kc), F32), pltpu.VMEM((t // kc, kc, tq), F32),
                        pltpu.VMEM((tq, WIDTH), BF16),
                        pltpu.VMEM((t // kc, N_HEADS, tq, kc), F32), pltpu.VMEM((N_HEADS, tq, LANES), F32),
                        pltpu.VMEM((N_HEADS, tq, 1), F32), pltpu.VMEM((N_HEADS, tq, ACC_W), F32)],
        compiler_params=_cparams("arbitrary", "arbitrary"), name="dsa_prompt",
    )(pp["q16"], pp["q16"], pp["wi"], pp["kit16"], pp["kvat16"])


def _top_blocks(gate, valid, axis):
    idx = lax.broadcasted_iota(jnp.int32, gate.shape, axis).astype(F32)
    g = jnp.where(valid, gate, -jnp.inf)
    sel = jnp.zeros(gate.shape, F32)
    for _ in range(MOBA_TOPK):
        m = jnp.max(g, axis=axis, keepdims=True)
        first = jnp.min(jnp.where(g == m, idx, float(gate.shape[axis])), axis=axis, keepdims=True)
        pick = idx == first
        sel = jnp.where(jnp.logical_and(pick, m > -jnp.inf), 1.0, sel)
        g = jnp.where(pick, -jnp.inf, g)
    return sel


def _block_indicator(n_rows, n_cols, scale):
    blk = lax.broadcasted_iota(jnp.int32, (n_rows, n_cols), 1) // MOBA_BLOCK
    return jnp.where(blk == lax.broadcasted_iota(jnp.int32, (n_rows, n_cols), 0), scale, 0.0).astype(BF16)


def _moba_prompt_kernel(qb_ref, kvt_ref, o_ref, kext_ref, kmean_ref, qe_ref, s_ref, mx_ref, m_ref, acc_ref):
    own = pl.program_id(1)
    t = kvt_ref.shape[3]
    tq = qb_ref.shape[0]
    n_ext = kext_ref.shape[1] - HEAD_DIM

    @pl.when(own == 0)
    def _():
        ind = _block_indicator(n_ext, t, 1.0)
        for g in range(N_KV):
            kt = kvt_ref[0, 2 * g]
            kext_ref[g, :HEAD_DIM, :] = kt
            kext_ref[g, HEAD_DIM:, :] = ind
            kmean_ref[g] = lax.dot_general(ind, kt, _NT, preferred_element_type=F32) * (1.0 / MOBA_BLOCK)

    blk = lax.broadcasted_iota(jnp.int32, (n_ext, tq), 0)
    for h in range(N_HEADS):
        g = h // GROUP
        q = qb_ref[:, h * HEAD_DIM:(h + 1) * HEAD_DIM]
        gate_t = lax.dot_general(kmean_ref[g].astype(BF16), q, _NT, preferred_element_type=F32)
        picked_t = _top_blocks(gate_t, blk < own, 0)
        allowed = jnp.logical_or(picked_t > 0.5, blk == own)
        bias = _pad_rows(jnp.where(allowed, 0.0, MASKED), LANES).T[:, :n_ext]
        qe_ref[h] = jnp.concatenate([q * SM_SCALE, bias.astype(BF16)], axis=1)

    mx_ref[...] = jnp.full(mx_ref.shape, MASKED, F32)
    acc_ref[...] = jnp.zeros(acc_ref.shape, F32)

    def logits(n, extra_bias):
        k0 = pl.multiple_of(n * MOBA_BLOCK, MOBA_BLOCK)
        for g in range(N_KV):
            ke = kext_ref[g, :, pl.ds(k0, MOBA_BLOCK)]
            for r in range(GROUP):
                h = g * GROUP + r
                s = jnp.dot(qe_ref[h], ke, preferred_element_type=F32)
                _logits_step(n, h, s if extra_bias is None else s + extra_bias, s_ref, mx_ref)

    def logits_body(n, carry):
        logits(n, None)
        return carry

    lax.fori_loop(0, own, logits_body, 0)
    causal = (lax.broadcasted_iota(jnp.int32, (tq, MOBA_BLOCK), 1)
              <= lax.broadcasted_iota(jnp.int32, (tq, MOBA_BLOCK), 0))
    logits(own, jnp.where(causal, 0.0, MASKED))
    _row_max(mx_ref, m_ref)

    def values_body(n, carry):
        k0 = pl.multiple_of(n * MOBA_BLOCK, MOBA_BLOCK)
        for g in range(N_KV):
            vt = _with_ones(kvt_ref[0, 2 * g + 1, :, pl.ds(k0, MOBA_BLOCK)])
            for r in range(GROUP):
                _values_step(n, g * GROUP + r, vt, s_ref, m_ref, acc_ref)
        return carry

    lax.fori_loop(0, own + 1, values_body, 0)
    _softmax_finish(o_ref, acc_ref)


def _moba_prompt(pp, b, t):
    tq = MOBA_BLOCK
    n_t = t // tq
    assert t % MOBA_BLOCK == 0 and n_t <= LANES
    n_ext = -(-n_t // BF16_ROWS) * BF16_ROWS
    return pl.pallas_call(
        _moba_prompt_kernel, grid=(b, n_t),
        in_specs=[pl.BlockSpec((tq, WIDTH), lambda bb, i: (bb * n_t + i, 2)),
                  pl.BlockSpec((1, KV_SLABS, HEAD_DIM, t), lambda bb, i: (bb, 0, 0, 0))],
        out_specs=pl.BlockSpec((tq, WIDTH), lambda bb, i: (bb * n_t + i, 0)),
        out_shape=jax.ShapeDtypeStruct((b * t, WIDTH), BF16),
        scratch_shapes=[pltpu.VMEM((N_KV, HEAD_DIM + n_ext, t), BF16), pltpu.VMEM((N_KV, n_ext, HEAD_DIM), F32),
                        pltpu.VMEM((N_HEADS, tq, HEAD_DIM + n_ext), BF16),
                        pltpu.VMEM((n_t, N_HEADS, tq, MOBA_BLOCK), F32), pltpu.VMEM((N_HEADS, tq, LANES), F32),
                        pltpu.VMEM((N_HEADS, tq, 1), F32), pltpu.VMEM((N_HEADS, tq, ACC_W), F32)],
        compiler_params=_cparams("arbitrary", "arbitrary"), name="moba_prompt",
    )(pp["q16"], pp["kvbt16"])


SEQ_PER_STEP = 2
DECODE_CHUNK = 2048


def _page_copy(pt_ref, pool, buf, sems, j, seq, slot, u, p):
    off = pl.multiple_of(p * PAGE_SIZE, PAGE_SIZE)
    lead = (slice(None),) * (len(buf.shape) - 3)
    return pltpu.make_async_copy(pool.at[pt_ref[seq, p]], buf.at[(slot, u) + lead + (pl.ds(off, PAGE_SIZE),)],
                                 sems.at[2 * j + slot])


def _paged_fetch(pt_ref, pools, bufs, sems, past):
    b = pl.program_id(0)
    nb = pl.num_programs(0)
    slot = b % 2
    n_pages = past // PAGE_SIZE

    def each_copy(step, sl, p, fn):
        for u in range(SEQ_PER_STEP):
            for j, (pool, buf) in enumerate(zip(pools, bufs)):
                fn(_page_copy(pt_ref, pool, buf, sems, j, step * SEQ_PER_STEP + u, sl, u, p))

    def start(step, sl):
        def body(p, carry):
            each_copy(step, sl, p, lambda cp: cp.start())
            return carry
        lax.fori_loop(0, n_pages, body, 0)

    @pl.when(b == 0)
    def _():
        start(0, 0)

    @pl.when(b + 1 < nb)
    def _():
        start(b + 1, 1 - slot)

    def wait_body(p, carry):
        each_copy(b, slot, p, lambda cp: cp.wait())
        return carry
    lax.fori_loop(0, n_pages, wait_body, 0)
    return slot


def _stack_heads(q, heads, scale=1.0):
    qf = q.astype(F32) * scale
    return jnp.concatenate([qf[:, h * HEAD_DIM:(h + 1) * HEAD_DIM] for h in heads], axis=0).astype(BF16)


def _pad_rows(x, n):
    return jnp.concatenate([x, jnp.zeros((n - x.shape[0], x.shape[1]), x.dtype)], axis=0)


def _chunked_attention(queries, kt_of, vt_of, bias_of, bias_new, k_new, v_new, s_ref, chunk):
    n = len(queries)
    rows = queries[0].shape[0]
    n_chunks = s_ref.shape[2] // chunk

    def logits_body(c, mx):
        k0 = pl.multiple_of(c * chunk, chunk)
        out = []
        for i in range(n):
            s = jnp.dot(queries[i], kt_of(i, k0), preferred_element_type=F32) + bias_of(i, k0)
            s_ref[i, :, pl.ds(k0, chunk)] = s
            out.append(jnp.maximum(mx[i], _fold_lanes(s, jnp.maximum)))
        return tuple(out)

    mx = lax.fori_loop(0, n_chunks, logits_body, tuple(jnp.full((rows, LANES), MASKED, F32) for _ in range(n)))
    s_new = [lax.dot_general(queries[i], k_new[i], _NT, preferred_element_type=F32) + bias_new[i] for i in range(n)]
    m = [jnp.maximum(jnp.max(mx[i], axis=1, keepdims=True), jnp.max(s_new[i], axis=1, keepdims=True))
         for i in range(n)]

    def values_body(c, carry):
        k0 = pl.multiple_of(c * chunk, chunk)
        out = []
        for i in range(n):
            l, acc = carry[i]
            p = jnp.exp(s_ref[i, :, pl.ds(k0, chunk)] - m[i])
            out.append((l + _fold_lanes(p, jnp.add),
                        acc + lax.dot_general(p.astype(BF16), vt_of(i, k0), _NT, preferred_element_type=F32)))
        return tuple(out)

    init = tuple((jnp.zeros((rows, LANES), F32), jnp.zeros((rows, HEAD_DIM), F32)) for _ in range(n))
    carry = lax.fori_loop(0, n_chunks, values_body, init)
    outs = []
    for i in range(n):
        l, acc = carry[i]
        p_new = jnp.exp(s_new[i] - m[i])
        l = jnp.sum(l, axis=1, keepdims=True) + jnp.sum(p_new, axis=1, keepdims=True)
        acc = acc + jnp.dot(p_new.astype(BF16), v_new[i], preferred_element_type=F32)
        outs.append(acc / l)
    return outs


def _store_heads(o_ref, u, per_group, nq):
    outs = [per_group[g][r * nq:(r + 1) * nq, :] for g in range(N_KV) for r in range(GROUP)]
    o_ref[u] = jnp.concatenate(outs, axis=1).astype(o_ref.dtype)


def _new_kv(kv_new, g):
    k = kv_new[:, 2 * g * HEAD_DIM:(2 * g + 1) * HEAD_DIM]
    v = kv_new[:, (2 * g + 1) * HEAD_DIM:(2 * g + 2) * HEAD_DIM]
    return _pad_rows(k, LANES).astype(BF16), _pad_rows(v, LANES).astype(BF16)


def _units():
    return [(u, g) for u in range(SEQ_PER_STEP) for g in range(N_KV)]


def _dsa_decode_kernel(pt_ref, q_ref, kiw_ref, kva_ref, idx_pool, kv_pool, o_ref,
                       idxbuf, kvbuf, score_ref, s_ref, sems, *, past, k_sel, chunk):
    nq = q_ref.shape[1]
    n_chunks = past // chunk
    slot = _paged_fetch(pt_ref, (idx_pool, kv_pool), (idxbuf, kvbuf), sems, past)
    col_new = lax.broadcasted_iota(jnp.int32, (1, LANES), 1)
    qidx = lax.broadcasted_iota(jnp.int32, (nq, 1), 0)

    qi_rows = [_stack_heads(q_ref[u][:, WIDTH:2 * WIDTH], range(IDX_HEADS)) for u in range(SEQ_PER_STEP)]
    kiw = [kiw_ref[u] for u in range(SEQ_PER_STEP)]

    def head_sum(s, u):
        score = jnp.zeros((nq, s.shape[1]), F32)
        for h in range(IDX_HEADS):
            score = score + jnp.maximum(s[h * nq:(h + 1) * nq, :], 0.0) * kiw[u][:, IDX_DIM + h:IDX_DIM + h + 1]
        return score

    def idx_body(c, carry):
        k0 = pl.multiple_of(c * chunk, chunk)
        for u in range(SEQ_PER_STEP):
            keys = idxbuf[slot, u, :, pl.ds(k0, chunk)].astype(BF16)
            score_ref[0, u * nq:(u + 1) * nq, pl.ds(k0, chunk)] = head_sum(
                jnp.dot(qi_rows[u], keys, preferred_element_type=F32), u)
        return carry

    lax.fori_loop(0, n_chunks, idx_body, 0)
    for u in range(SEQ_PER_STEP):
        k_new = _pad_rows(kiw[u][:, :IDX_DIM], LANES).astype(BF16)
        s = head_sum(lax.dot_general(qi_rows[u], k_new, _NT, preferred_element_type=F32), u)
        score_ref[0, u * nq:(u + 1) * nq, past:] = jnp.where(col_new <= qidx, s, -jnp.inf)

    n_valid = jnp.concatenate([past + qidx + 1] * SEQ_PER_STEP, axis=0)
    thr, jthr = _select_threshold(score_ref, 1, n_valid, k_sel, 1)

    def bias_rows(u, k0, width):
        x = score_ref[0, u * nq:(u + 1) * nq, pl.ds(k0, width)]
        t, j = thr[u * nq:(u + 1) * nq], jthr[u * nq:(u + 1) * nq]
        col = (k0 + lax.broadcasted_iota(jnp.int32, (1, width), 1)).astype(F32)
        sel = jnp.logical_or(x > t, jnp.logical_and(x == t, col <= j))
        return jnp.concatenate([jnp.where(sel, 0.0, MASKED)] * GROUP, axis=0)

    units = _units()
    queries = [_stack_heads(q_ref[u][:, :WIDTH], range(g * GROUP, (g + 1) * GROUP), SM_SCALE) for u, g in units]
    new = [_new_kv(kva_ref[u], g) for u, g in units]
    outs = _chunked_attention(
        queries,
        lambda i, k0: kvbuf[slot, units[i][0], 2 * units[i][1], :, pl.ds(k0, chunk)].astype(BF16),
        lambda i, k0: kvbuf[slot, units[i][0], 2 * units[i][1] + 1, :, pl.ds(k0, chunk)].astype(BF16),
        lambda i, k0: bias_rows(units[i][0], k0, chunk),
        [bias_rows(u, past, LANES) for u, g in units], [kv[0] for kv in new], [kv[1] for kv in new],
        s_ref, chunk)
    for u in range(SEQ_PER_STEP):
        _store_heads(o_ref, u, outs[u * N_KV:(u + 1) * N_KV], nq)


def _moba_decode_kernel(pt_ref, q_ref, kvb_ref, kv_pool, o_ref, kvbuf, s_ref, sems, *, past, chunk):
    nq = q_ref.shape[1]
    n_blk = past // MOBA_BLOCK
    n_chunks = past // chunk
    slot = _paged_fetch(pt_ref, (kv_pool,), (kvbuf,), sems, past)
    rows = GROUP * nq
    units = _units()
    kt_of = lambda i, k0: kvbuf[slot, units[i][0], 2 * units[i][1], :, pl.ds(k0, chunk)].astype(BF16)

    def indicator(k0):
        blk = (k0 + lax.broadcasted_iota(jnp.int32, (n_blk, chunk), 1)) // MOBA_BLOCK
        return jnp.where(blk == lax.broadcasted_iota(jnp.int32, (n_blk, chunk), 0), 1.0, 0.0).astype(BF16)

    def mean_body(c, sums):
        k0 = pl.multiple_of(c * chunk, chunk)
        ind = indicator(k0)
        return tuple(sums[i] + lax.dot_general(ind, kt_of(i, k0), _NT, preferred_element_type=F32)
                     for i in range(len(units)))

    sums = lax.fori_loop(0, n_chunks, mean_body, tuple(jnp.zeros((n_blk, HEAD_DIM), F32) for _ in units))

    queries, bias_blk = [], []
    for i, (u, g) in enumerate(units):
        qb = q_ref[u][:, 2 * WIDTH:3 * WIDTH]
        kmean = (sums[i] * (1.0 / MOBA_BLOCK)).astype(BF16)
        gate = lax.dot_general(_stack_heads(qb, range(g * GROUP, (g + 1) * GROUP)), kmean, _NT,
                               preferred_element_type=F32)
        picked = _top_blocks(gate, jnp.full(gate.shape, True), 1)
        bias_blk.append(jnp.where(picked > 0.5, 0.0, MASKED).astype(BF16))
        queries.append(_stack_heads(qb, range(g * GROUP, (g + 1) * GROUP), SM_SCALE))

    col = lax.broadcasted_iota(jnp.int32, (rows, LANES), 1)
    qidx = lax.broadcasted_iota(jnp.int32, (rows, LANES), 0) % nq
    bias_new = jnp.where(col <= qidx, 0.0, MASKED)
    new = [_new_kv(kvb_ref[u], g) for u, g in units]
    outs = _chunked_attention(
        queries, kt_of,
        lambda i, k0: kvbuf[slot, units[i][0], 2 * units[i][1] + 1, :, pl.ds(k0, chunk)].astype(BF16),
        lambda i, k0: jnp.dot(bias_blk[i], indicator(k0), preferred_element_type=F32),
        [bias_new] * len(units), [kv[0] for kv in new], [kv[1] for kv in new], s_ref, chunk)
    for u in range(SEQ_PER_STEP):
        _store_heads(o_ref, u, outs[u * N_KV:(u + 1) * N_KV], nq)


def _sample_specs(nq, widths):
    return [pl.BlockSpec((SEQ_PER_STEP, nq, w), lambda bb, pt: (bb, 0, 0)) for w in widths]


def _dsa_decode(ps, page_table, idx_pool_t, kv_pool_t, db, nq):
    past = page_table.shape[1] * PAGE_SIZE
    lp = past + LANES
    k_sel = min(IDX_TOPK, (past + nq) // 4)
    qw = ps["q16"].shape[1]
    kern = functools.partial(_dsa_decode_kernel, past=past, k_sel=k_sel, chunk=math.gcd(past, DECODE_CHUNK))
    grid_spec = pltpu.PrefetchScalarGridSpec(
        num_scalar_prefetch=1, grid=(db // SEQ_PER_STEP,),
        in_specs=_sample_specs(nq, [qw, LANES, KV_WIDTH])
        + [pl.BlockSpec(memory_space=pl.ANY), pl.BlockSpec(memory_space=pl.ANY)],
        out_specs=_sample_specs(nq, [WIDTH])[0],
        scratch_shapes=[pltpu.VMEM((2, SEQ_PER_STEP, IDX_DIM, past), F32),
                        pltpu.VMEM((2, SEQ_PER_STEP, KV_SLABS, HEAD_DIM, past), F32),
                        pltpu.VMEM((1, SEQ_PER_STEP * nq, lp), F32),
                        pltpu.VMEM((SEQ_PER_STEP * N_KV, GROUP * nq, past), F32), pltpu.SemaphoreType.DMA((4,))])
    return pl.pallas_call(
        kern, grid_spec=grid_spec, out_shape=jax.ShapeDtypeStruct((db, nq, WIDTH), BF16),
        compiler_params=_cparams("arbitrary"), name="dsa_decode",
    )(page_table, ps["q16"].reshape(db, nq, qw), ps["kiw"].reshape(db, nq, LANES),
      ps["kva"].reshape(db, nq, KV_WIDTH), idx_pool_t, kv_pool_t)


def _moba_decode(ps, page_table, kv_pool_t, db, nq):
    past = page_table.shape[1] * PAGE_SIZE
    assert past % MOBA_BLOCK == 0 and nq <= MOBA_BLOCK and db % SEQ_PER_STEP == 0
    qw = ps["q16"].shape[1]
    kern = functools.partial(_moba_decode_kernel, past=past, chunk=math.gcd(past, DECODE_CHUNK))
    grid_spec = pltpu.PrefetchScalarGridSpec(
        num_scalar_prefetch=1, grid=(db // SEQ_PER_STEP,),
        in_specs=_sample_specs(nq, [qw, KV_WIDTH]) + [pl.BlockSpec(memory_space=pl.ANY)],
        out_specs=_sample_specs(nq, [WIDTH])[0],
        scratch_shapes=[pltpu.VMEM((2, SEQ_PER_STEP, KV_SLABS, HEAD_DIM, past), F32),
                        pltpu.VMEM((SEQ_PER_STEP * N_KV, GROUP * nq, past), F32), pltpu.SemaphoreType.DMA((2,))])
    return pl.pallas_call(
        kern, grid_spec=grid_spec, out_shape=jax.ShapeDtypeStruct((db, nq, WIDTH), BF16),
        compiler_params=_cparams("arbitrary"), name="moba_decode",
    )(page_table, ps["q16"].reshape(db, nq, qw), ps["kvb"].reshape(db, nq, KV_WIDTH), kv_pool_t)


def _out_kernel(x_ref, oa_ref, ob_ref, gate_ref, wba_ref, wbb_ref, wo_ref, y_ref):
    d = x_ref.shape[1]
    ua = oa_ref[...] * gate_ref[:, 0:WIDTH]
    ub = ob_ref[...] * gate_ref[:, WIDTH:2 * WIDTH]
    a = jnp.dot(ua, wba_ref[...], preferred_element_type=F32)
    b = jnp.dot(ub, wbb_ref[...], preferred_element_type=F32)
    ga = gate_ref[:, 2 * WIDTH:2 * WIDTH + d].astype(F32)
    gb = gate_ref[:, 2 * WIDTH + d:2 * WIDTH + 2 * d].astype(F32)
    merged = (ga * a + gb * b).astype(BF16)
    y_ref[...] = x_ref[...] + jnp.dot(merged, wo_ref[...], preferred_element_type=F32)


def _out_proj(x2d, oa, ob, gate16, wts, tm):
    n, d = x2d.shape
    row = lambda w: pl.BlockSpec((tm, w), lambda i: (i, 0))
    return pl.pallas_call(
        _out_kernel, grid=(n // tm,),
        in_specs=[row(d), row(WIDTH), row(WIDTH), row(gate16.shape[1]),
                  _const_spec((WIDTH, d)), _const_spec((WIDTH, d)), _const_spec((d, d))],
        out_specs=row(d), out_shape=jax.ShapeDtypeStruct((n, d), F32),
        compiler_params=_cparams("arbitrary"), name="out_proj",
    )(x2d, oa, ob, gate16, wts["wba"], wts["wbb"], wts["wo"])


def _rope_angles(pos):
    inv = ROPE_THETA ** (-jnp.arange(ROT_HALF, dtype=F32) / ROT_HALF)
    ang = pos.astype(F32)[:, None] * inv[None, :]
    return jnp.cos(ang), jnp.sin(ang)


def _rope_tables(pos):
    p = pos.shape[0]
    c, s = _rope_angles(pos)
    rest = HEAD_DIM - 2 * ROT_HALF
    z8, zr = jnp.zeros((p, ROT_HALF), F32), jnp.zeros((p, rest), F32)
    c_head = jnp.concatenate([c, c, jnp.ones((p, rest), F32)], axis=1)
    s1_head = jnp.concatenate([-s, z8, zr], axis=1)
    s2_head = jnp.concatenate([z8, s, zr], axis=1)
    one, zero = jnp.ones((p, HEAD_DIM), F32), jnp.zeros((p, HEAD_DIM), F32)
    cat = lambda a, b: jnp.concatenate([a, b], axis=1)
    return (cat(c_head, c_head), cat(s1_head, s1_head), cat(s2_head, s2_head),
            cat(c_head, one), cat(s1_head, zero), cat(s2_head, zero))


def _prep_weights(norm_g, w_in, qn_a, kn_a, qn_i, kn_i, qn_b, kn_b, w_ba, w_bb, w_out):
    d = w_in.shape[0]
    kvw = N_KV * HEAD_DIM
    splits = (WIDTH, kvw, kvw, IDX_HEADS * IDX_DIM, IDX_DIM, IDX_HEADS, WIDTH, WIDTH, kvw, kvw, WIDTH, d, d)
    offs = np.concatenate([[0], np.cumsum(splits)])
    qa, ka, va, qi, ki, wi, za, qb, kb, vb, zb, ga, gb = [w_in[:, offs[j]:offs[j + 1]] for j in range(13)]
    hd = HEAD_DIM
    pad = jnp.zeros((d, LANES - IDX_DIM - IDX_HEADS), F32)
    slabs = jnp.concatenate([ka[:, :hd], va[:, :hd], ka[:, hd:], va[:, hd:],
                             kb[:, :hd], vb[:, :hd], kb[:, hd:], vb[:, hd:], ki], axis=1)
    wk = jnp.concatenate([slabs, wi, pad], axis=1)
    wwi = jnp.concatenate([wi, jnp.zeros((d, LANES - IDX_HEADS), F32)], axis=1)
    zero = jnp.zeros((hd,), F32)
    one = jnp.ones((hd,), F32)
    wi_gain = jnp.concatenate([jnp.full((IDX_HEADS,), (IDX_HEADS * IDX_DIM) ** -0.5, F32),
                               jnp.zeros((hd - IDX_HEADS,), F32)])
    gka = jnp.concatenate([kn_a, zero, kn_a, zero, kn_b, zero, kn_b, zero, kn_i, zero])[None, :]
    gkb = jnp.concatenate([zero, one, zero, one, zero, one, zero, one, zero, wi_gain])[None, :]
    gkt = jnp.concatenate([kn_a, one, kn_a, one, kn_b, one, kn_b, one, kn_i])[:, None]
    gq = jnp.concatenate([jnp.tile(qn_a, N_HEADS), jnp.tile(qn_i, IDX_HEADS), jnp.tile(qn_b, N_HEADS)])[None, :]
    blk = np.arange(MXU_DIM) // HEAD_DIM
    bd = jnp.asarray((blk[:, None] == blk[None, :]) / HEAD_DIM, BF16)
    return dict(
        ng=norm_g[None, :].astype(F32),
        wq=jnp.concatenate([qa, qi, qb], axis=1).astype(BF16), wk=wk.astype(BF16),
        wkt=slabs.T.astype(BF16), wwi=wwi.astype(BF16),
        wg=jnp.concatenate([za, zb, ga, gb], axis=1).astype(BF16),
        gq=gq, gka=gka, gkb=gkb, gkt=gkt, bd=bd,
        wba=w_ba.astype(BF16), wbb=w_bb.astype(BF16), wo=w_out.astype(BF16))


def kernel(x_prompt, x_sample, cache_kv_a, cache_idx_k, cache_kv_b, page_table, norm_g, w_in, q_norm_a, k_norm_a,
           idx_q_norm, idx_k_norm, q_norm_b, k_norm_b, w_branch_a, w_branch_b, w_out):
    b, t, d = x_prompt.shape
    db, nq, _ = x_sample.shape
    depth = w_in.shape[0]
    n_phys = cache_kv_a.shape[1]
    past = page_table.shape[1] * PAGE_SIZE
    tm = 256
    tm_s = min(tm, db * nq)
    assert t % tm == 0 and tm_s % nq == 0 and (db * nq) % tm_s == 0

    pos_p = jnp.arange(t, dtype=jnp.int32)
    cos_p, sin_p = _rope_angles(pos_p)
    tab_p = _rope_tables(pos_p)[:3] + (cos_p.T, sin_p.T)
    tab_s = tuple(jnp.tile(a, (tm_s // nq, 1)) for a in _rope_tables(past + jnp.arange(nq, dtype=jnp.int32)))

    kv_t = lambda pool: jnp.transpose(pool, (0, 2, 3, 4, 1)).reshape(n_phys, KV_SLABS, HEAD_DIM, PAGE_SIZE)

    hp = x_prompt.reshape(b * t, d)
    hs = x_sample.reshape(db * nq, d)
    new = [[] for _ in range(6)]
    for l in range(depth):
        wts = _prep_weights(norm_g[l], w_in[l], q_norm_a[l], k_norm_a[l], idx_q_norm[l], idx_k_norm[l],
                            q_norm_b[l], k_norm_b[l], w_branch_a[l], w_branch_b[l], w_out[l])
        pp = _project_prompt(hp, tab_p, wts, b, t, tm)
        oa = _dsa_prompt(pp, b, t, tq=256, kc=min(512, t))
        ob = _moba_prompt(pp, b, t)
        hp = _out_proj(hp, oa, ob, pp["gate16"], wts, tm)

        ps = _project_sample(hs, tab_s, wts, tm_s)
        oa_s = _dsa_decode(ps, page_table, jnp.transpose(cache_idx_k[l], (0, 2, 1)), kv_t(cache_kv_a[l]), db, nq)
        ob_s = _moba_decode(ps, page_table, kv_t(cache_kv_b[l]), db, nq)
        hs = _out_proj(hs, oa_s.reshape(db * nq, WIDTH), ob_s.reshape(db * nq, WIDTH), ps["gate16"], wts, tm_s)

        to_tokens = lambda a: jnp.transpose(a.reshape(b, N_KV, 2, HEAD_DIM, t), (0, 4, 1, 2, 3))
        new[0].append(to_tokens(pp["kvat"]))
        new[1].append(jnp.transpose(pp["kit"], (0, 2, 1)))
        new[2].append(to_tokens(pp["kvbt"]))
        new[3].append(ps["kva"].reshape(db, nq, N_KV, 2, HEAD_DIM))
        new[4].append(ps["kiw"][:, :IDX_DIM].reshape(db, nq, IDX_DIM))
        new[5].append(ps["kvb"].reshape(db, nq, N_KV, 2, HEAD_DIM))
    return (hp.reshape(b, t, d), hs.reshape(db, nq, d)) + tuple(jnp.stack(a, axis=0) for a in new)
```

```python
import functools
import math

import numpy as np
import jax
import jax.numpy as jnp
from jax import lax
from jax.experimental import pallas as pl
from jax.experimental.pallas import tpu as pltpu

F32 = jnp.float32
BF16 = jnp.bfloat16

HEAD_DIM = 64
N_HEADS = 8
N_KV = 2
GROUP = N_HEADS // N_KV
WIDTH = N_HEADS * HEAD_DIM
KV_WIDTH = N_KV * 2 * HEAD_DIM
KV_SLABS = N_KV * 2
IDX_HEADS = 8
IDX_DIM = 64
IDX_TOPK = 256
MOBA_BLOCK = 256
MOBA_TOPK = 3
PAGE_SIZE = 128
ROPE_THETA = 500000.0
ROT_HALF = HEAD_DIM // 4 // 2
RMS_EPS = 1e-6
SM_SCALE = HEAD_DIM ** -0.5

LANES = 128
SUBLANES = 8
MXU_DIM = 256
BF16_ROWS = 16
VMEM_LIMIT_BYTES = 56 * 1024 * 1024

MASKED = -1e30
BISECT_STEPS = 6

_NT = (((1,), (1,)), ((), ()))


def _cparams(*sem):
    return pltpu.CompilerParams(dimension_semantics=sem, vmem_limit_bytes=VMEM_LIMIT_BYTES)


def _normed_input(x_ref, ng_ref):
    x = x_ref[...]
    inv = lax.rsqrt(jnp.mean(x * x, axis=-1, keepdims=True) + RMS_EPS)
    return (x * inv * ng_ref[...]).astype(BF16)


def _head_mean_sq(z, bd_ref):
    parts = []
    for c in range(0, z.shape[1], MXU_DIM):
        cw = min(MXU_DIM, z.shape[1] - c)
        zc = z[:, c:c + cw]
        parts.append(jnp.dot((zc * zc).astype(BF16), bd_ref[:cw, :cw], preferred_element_type=F32))
    return parts[0] if len(parts) == 1 else jnp.concatenate(parts, axis=1)


def _rope_lanes(y, c_ref, s1_ref, s2_ref):
    return (y * c_ref[...] + pltpu.roll(y, LANES - ROT_HALF, 1) * s1_ref[...]
            + pltpu.roll(y, ROT_HALF, 1) * s2_ref[...])


def _queries_and_gates(h, wq_ref, wg_ref, gq_ref, bd_ref, cq_ref, s1q_ref, s2q_ref, q16_ref, gate16_ref):
    for c0 in range(0, q16_ref.shape[1], 512):
        z = jnp.dot(h, wq_ref[:, c0:c0 + 512], preferred_element_type=F32)
        y = z * lax.rsqrt(_head_mean_sq(z, bd_ref) + RMS_EPS) * gq_ref[:, c0:c0 + 512]
        for j in range(0, 512, LANES):
            q16_ref[:, c0 + j:c0 + j + LANES] = _rope_lanes(y[:, j:j + LANES], cq_ref, s1q_ref, s2q_ref).astype(BF16)
    n_silu = 2 * WIDTH
    for c0 in range(0, gate16_ref.shape[1], 512):
        z = jnp.dot(h, wg_ref[:, c0:c0 + 512], preferred_element_type=F32)
        sig = 1.0 / (1.0 + jnp.exp(-z))
        gate16_ref[:, c0:c0 + 512] = (z * sig if c0 < n_silu else sig).astype(BF16)


def _proj_sample_kernel(x_ref, ng_ref, wq_ref, wk_ref, wg_ref, gq_ref, gka_ref, gkb_ref, bd_ref,
                        cq_ref, s1q_ref, s2q_ref, ck_ref, s1k_ref, s2k_ref,
                        q16_ref, gate16_ref, kva_ref, kvb_ref, kiw_ref):
    h = _normed_input(x_ref, ng_ref)
    _queries_and_gates(h, wq_ref, wg_ref, gq_ref, bd_ref, cq_ref, s1q_ref, s2q_ref, q16_ref, gate16_ref)
    z = jnp.dot(h, wk_ref[...], preferred_element_type=F32)
    y = z * (lax.rsqrt(_head_mean_sq(z, bd_ref) + RMS_EPS) * gka_ref[...] + gkb_ref[...])
    r = [_rope_lanes(y[:, j * LANES:(j + 1) * LANES], ck_ref, s1k_ref, s2k_ref) for j in range(5)]
    for j in range(2):
        kva_ref[:, j * LANES:(j + 1) * LANES] = r[j]
        kvb_ref[:, j * LANES:(j + 1) * LANES] = r[2 + j]
    kiw_ref[...] = r[4]


def _proj_prompt_kernel(x_ref, ng_ref, wq_ref, wkt_ref, wwi_ref, wg_ref, gq_ref, gkt_ref, bd_ref,
                        cq_ref, s1q_ref, s2q_ref, ct_ref, st_ref,
                        q16_ref, gate16_ref, wi_ref, kvat_ref, kvbt_ref, kit_ref, kvat16_ref, kvbt16_ref, kit16_ref):
    h = _normed_input(x_ref, ng_ref)
    _queries_and_gates(h, wq_ref, wg_ref, gq_ref, bd_ref, cq_ref, s1q_ref, s2q_ref, q16_ref, gate16_ref)
    wi_ref[...] = jnp.dot(h, wwi_ref[...], preferred_element_type=F32) * (IDX_HEADS * IDX_DIM) ** -0.5

    zt = lax.dot_general(wkt_ref[...], h, _NT, preferred_element_type=F32)
    cos, sin = ct_ref[...], st_ref[...]

    def key_head(j):
        z = zt[j * HEAD_DIM:(j + 1) * HEAD_DIM, :]
        y = z * lax.rsqrt(jnp.mean(z * z, axis=0, keepdims=True) + RMS_EPS) * gkt_ref[j * HEAD_DIM:(j + 1) * HEAD_DIM, :]
        y1, y2 = y[:ROT_HALF], y[ROT_HALF:2 * ROT_HALF]
        return jnp.concatenate([y1 * cos - y2 * sin, y2 * cos + y1 * sin, y[2 * ROT_HALF:]], axis=0)

    for j in range(KV_SLABS):
        for base, out_ref, out16_ref in ((0, kvat_ref, kvat16_ref), (KV_SLABS, kvbt_ref, kvbt16_ref)):
            slab = key_head(base + j) if j % 2 == 0 else zt[(base + j) * HEAD_DIM:(base + j + 1) * HEAD_DIM, :]
            out_ref[0, j] = slab
            out16_ref[0, j] = slab.astype(BF16)
    ki = key_head(2 * KV_SLABS)
    kit_ref[0] = ki
    kit16_ref[0] = ki.astype(BF16)


def _const_spec(shape, n_grid=1):
    if n_grid == 1:
        return pl.BlockSpec(shape, lambda i: (0,) * len(shape))
    return pl.BlockSpec(shape, lambda b, i: (0,) * len(shape))


def _project_sample(x2d, tables, wts, tm):
    n, d = x2d.shape
    wq, wk, wg = wts["wq"], wts["wk"], wts["wg"]
    row = lambda w: pl.BlockSpec((tm, w), lambda i: (i, 0))
    tab = pl.BlockSpec((tm, LANES), lambda i: (0, 0))
    in_specs = [row(d), _const_spec((1, d)), _const_spec(wq.shape), _const_spec(wk.shape), _const_spec(wg.shape),
                _const_spec((1, wq.shape[1])), _const_spec((1, wk.shape[1])), _const_spec((1, wk.shape[1])),
                _const_spec((MXU_DIM, MXU_DIM))] + [tab] * 6
    out_shape = [
        jax.ShapeDtypeStruct((n, wq.shape[1]), BF16),
        jax.ShapeDtypeStruct((n, wg.shape[1]), BF16),
        jax.ShapeDtypeStruct((n, KV_WIDTH), F32),
        jax.ShapeDtypeStruct((n, KV_WIDTH), F32),
        jax.ShapeDtypeStruct((n, LANES), F32),
    ]
    outs = pl.pallas_call(
        _proj_sample_kernel, grid=(n // tm,), in_specs=in_specs, out_specs=[row(s.shape[1]) for s in out_shape],
        out_shape=out_shape, compiler_params=_cparams("arbitrary"), name="project_sample",
    )(x2d, wts["ng"], wq, wk, wg, wts["gq"], wts["gka"], wts["gkb"], wts["bd"], *tables)
    return dict(zip(("q16", "gate16", "kva", "kvb", "kiw"), outs))


def _project_prompt(x2d, tables, wts, b, t, tm):
    n, d = x2d.shape
    n_t = t // tm
    wq, wkt, wwi, wg = wts["wq"], wts["wkt"], wts["wwi"], wts["wg"]
    row = lambda w: pl.BlockSpec((tm, w), lambda bb, i: (bb * n_t + i, 0))
    tab = pl.BlockSpec((tm, LANES), lambda bb, i: (i, 0))
    tab_t = pl.BlockSpec((ROT_HALF, tm), lambda bb, i: (0, i))
    cs = lambda shape: _const_spec(shape, 2)
    in_specs = [row(d), cs((1, d)), cs(wq.shape), cs(wkt.shape), cs(wwi.shape), cs(wg.shape),
                cs((1, wq.shape[1])), cs((wkt.shape[0], 1)), cs((MXU_DIM, MXU_DIM)), tab, tab, tab, tab_t, tab_t]
    kv_t = lambda dt: jax.ShapeDtypeStruct((b, KV_SLABS, HEAD_DIM, t), dt)
    ki_t = lambda dt: jax.ShapeDtypeStruct((b, IDX_DIM, t), dt)
    out_shape = [jax.ShapeDtypeStruct((n, wq.shape[1]), BF16), jax.ShapeDtypeStruct((n, wg.shape[1]), BF16),
                 jax.ShapeDtypeStruct((n, LANES), F32),
                 kv_t(F32), kv_t(F32), ki_t(F32), kv_t(BF16), kv_t(BF16), ki_t(BF16)]
    kv_spec = pl.BlockSpec((1, KV_SLABS, HEAD_DIM, tm), lambda bb, i: (bb, 0, 0, i))
    ki_spec = pl.BlockSpec((1, IDX_DIM, tm), lambda bb, i: (bb, 0, i))
    out_specs = [row(wq.shape[1]), row(wg.shape[1]), row(LANES), kv_spec, kv_spec, ki_spec, kv_spec, kv_spec, ki_spec]
    outs = pl.pallas_call(
        _proj_prompt_kernel, grid=(b, n_t), in_specs=in_specs, out_specs=out_specs, out_shape=out_shape,
        compiler_params=_cparams("arbitrary", "arbitrary"), name="project_prompt",
    )(x2d, wts["ng"], wq, wkt, wwi, wg, wts["gq"], wts["gkt"], wts["bd"], *tables)
    return dict(zip(("q16", "gate16", "wi", "kvat", "kvbt", "kit", "kvat16", "kvbt16", "kit16"), outs))


def _fold_lanes(x, op):
    acc = x[:, :LANES]
    for j in range(1, x.shape[1] // LANES):
        acc = op(acc, x[:, j * LANES:(j + 1) * LANES])
    return acc


def _fold_sublanes(x, op):
    n = x.shape[0] // SUBLANES
    parts = [x[j * SUBLANES:(j + 1) * SUBLANES, :] for j in range(min(n, SUBLANES))]
    for j in range(len(parts), n):
        parts[j % SUBLANES] = op(parts[j % SUBLANES], x[j * SUBLANES:(j + 1) * SUBLANES, :])
    while len(parts) > 1:
        parts = [op(parts[j], parts[j + 1]) for j in range(0, len(parts) - 1, 2)] + parts[len(parts) & ~1:]
    return parts[0]


def _select_threshold(score_ref, n_chunks, n_valid, k, key_axis):
    width = score_ref.shape[1 + key_axis]
    rows = score_ref.shape[2 - key_axis]
    kf = float(k)
    neg, pos = -jnp.inf, jnp.inf
    fold = _fold_lanes if key_axis == 1 else _fold_sublanes
    acc_shape = (rows, LANES) if key_axis == 1 else (SUBLANES, rows)

    def reduce_all(fn, op, init, final_reduce):
        def body(c, acc):
            return op(acc, fold(fn(score_ref[c], c), op))
        acc = lax.fori_loop(0, n_chunks, body, jnp.full(acc_shape, init, F32))
        return final_reduce(acc, axis=key_axis, keepdims=True)

    def count(pred):
        return reduce_all(lambda x, c: jnp.where(pred(x, c), 1.0, 0.0), jnp.add, 0.0, jnp.sum)

    def max_below(v):
        return reduce_all(lambda x, c: jnp.where(x < v, x, neg), jnp.maximum, neg, jnp.max)

    small = n_valid <= k
    row_min = reduce_all(lambda x, c: jnp.where(x > neg, x, pos), jnp.minimum, pos, jnp.min)

    def cond(s):
        return jnp.sum(1.0 - s[5]) > 0.0

    def finished(c_lo, c_hi, done):
        return jnp.where(jnp.logical_or(c_hi == kf - 1.0, c_lo == kf), 1.0, done)

    def body(s):
        lo, hi, c_lo, c_hi, snap, done = s
        vmax = max_below(hi)
        c = count(lambda x, cc: x >= vmax)
        live = done < 0.5
        hit = jnp.logical_and(live, c >= kf)
        move = jnp.logical_and(live, c < kf)
        snap = jnp.where(hit, 1.0, snap)
        hi = jnp.where(move, vmax, hi)
        c_hi = jnp.where(move, c, c_hi)
        done = finished(c_lo, c_hi, jnp.where(hit, 1.0, done))
        for _ in range(BISECT_STEPS):
            mid = 0.5 * lo + 0.5 * hi
            c = count(lambda x, cc: x >= mid)
            live = done < 0.5
            up = jnp.logical_and(live, c >= kf)
            down = jnp.logical_and(live, c < kf)
            lo = jnp.where(up, mid, lo)
            c_lo = jnp.where(up, c, c_lo)
            hi = jnp.where(down, mid, hi)
            c_hi = jnp.where(down, c, c_hi)
            done = finished(c_lo, c_hi, done)
        return lo, hi, c_lo, c_hi, snap, done

    col = lambda v: jnp.full(n_valid.shape, v, F32)
    init = (row_min, col(pos), n_valid.astype(F32), col(0.0), col(0.0), jnp.where(small, 1.0, 0.0).astype(F32))
    lo, hi, c_lo, c_hi, snap, _ = lax.while_loop(cond, body, init)
    from_hi = jnp.logical_or(snap > 0.5, c_hi == kf - 1.0)
    above_lo = reduce_all(lambda x, c: jnp.where(x >= lo, x, pos), jnp.minimum, pos, jnp.min)
    thr = jnp.where(small, neg, jnp.where(from_hi, max_below(hi), above_lo))

    need = kf - count(lambda x, c: x > thr)
    n_eq = count(lambda x, c: x == thr)
    tie = jnp.logical_and(jnp.logical_not(small), n_eq > need)
    total = score_ref.shape[0] * width
    col0 = lax.broadcasted_iota(jnp.int32, (1, width) if key_axis == 1 else (width, 1), key_axis)

    def tie_search():
        lo = col(-1.0)
        hi = col(float(total - 1))
        for _ in range(int(np.ceil(np.log2(total))) + 1):
            mid = jnp.floor(0.5 * (lo + hi))
            ok = count(lambda x, c: jnp.logical_and(x == thr, (col0 + c * width).astype(F32) <= mid)) >= need
            hi = jnp.where(ok, mid, hi)
            lo = jnp.where(ok, lo, mid)
        return hi

    any_tie = jnp.sum(jnp.where(tie, 1.0, 0.0)) > 0.0
    jthr = lax.cond(any_tie, tie_search, lambda: col(float(total)))
    jthr = jnp.where(small, -1.0, jthr)
    return thr, jthr


ACC_W = HEAD_DIM + BF16_ROWS


def _with_ones(vt):
    return jnp.concatenate([vt, jnp.ones((BF16_ROWS, vt.shape[1]), vt.dtype)], axis=0)


def _logits_step(c, h, s, s_ref, mx_ref):
    s_ref[c, h] = s
    mx_ref[h] = jnp.maximum(mx_ref[h], _fold_lanes(s, jnp.maximum))


def _row_max(mx_ref, m_ref):
    for h in range(N_HEADS):
        m_ref[h] = jnp.max(mx_ref[h], axis=1, keepdims=True)


def _values_step(c, h, vt_ones, s_ref, m_ref, acc_ref):
    p = jnp.exp(s_ref[c, h] - m_ref[h])
    acc_ref[h] += lax.dot_general(p.astype(BF16), vt_ones, _NT, preferred_element_type=F32)


def _softmax_finish(o_ref, acc_ref):
    outs = []
    for h in range(N_HEADS):
        acc = acc_ref[h]
        outs.append(acc[:, :HEAD_DIM] / acc[:, HEAD_DIM:HEAD_DIM + 1])
    o_ref[...] = jnp.concatenate(outs, axis=1).astype(o_ref.dtype)


def _dsa_prompt_kernel(qa_ref, qi_ref, wi_ref, kit_ref, kvt_ref, o_ref,
                       score_ref, scoret_ref, qs_ref, s_ref, mx_ref, m_ref, acc_ref, *, k_sel):
    i = pl.program_id(1)
    _, tq, kc = score_ref.shape
    n_kc = (i * tq + tq + kc - 1) // kc
    rows = i * tq + lax.broadcasted_iota(jnp.int32, (tq, 1), 0)
    w = wi_ref[:, :IDX_HEADS]
    col0 = lax.broadcasted_iota(jnp.int32, (1, kc), 1)

    def idx_body(c, carry):
        k0 = pl.multiple_of(c * kc, kc)
        kblk = kit_ref[0, :, pl.ds(k0, kc)]
        acc = jnp.zeros((tq, kc), F32)
        for h in range(IDX_HEADS):
            s = jnp.dot(qi_ref[:, h * IDX_DIM:(h + 1) * IDX_DIM], kblk, preferred_element_type=F32)
            acc = acc + jnp.maximum(s, 0.0) * w[:, h:h + 1]
        masked = jnp.where(col0 + k0 <= rows, acc, -jnp.inf)
        score_ref[c] = masked
        scoret_ref[c] = masked.T
        return carry

    lax.fori_loop(0, n_kc, idx_body, 0)
    n_valid = i * tq + 1 + lax.broadcasted_iota(jnp.int32, (1, tq), 1)
    thr_t, jthr_t = _select_threshold(scoret_ref, n_kc, n_valid, k_sel, 0)
    to_column = lambda v: jnp.broadcast_to(v, (LANES, tq)).T[:, :1]
    thr, jthr = to_column(thr_t), to_column(jthr_t)

    qs_ref[...] = qa_ref[...] * SM_SCALE
    mx_ref[...] = jnp.full(mx_ref.shape, MASKED, F32)
    acc_ref[...] = jnp.zeros(acc_ref.shape, F32)

    def logits_body(c, carry):
        k0 = pl.multiple_of(c * kc, kc)
        x = score_ref[c]
        sel = jnp.logical_or(x > thr, jnp.logical_and(x == thr, (col0 + k0).astype(F32) <= jthr))
        bias = jnp.where(sel, 0.0, MASKED)
        for g in range(N_KV):
            kt = kvt_ref[0, 2 * g, :, pl.ds(k0, kc)]
            for r in range(GROUP):
                h = g * GROUP + r
                s = jnp.dot(qs_ref[:, h * HEAD_DIM:(h + 1) * HEAD_DIM], kt, preferred_element_type=F32) + bias
                _logits_step(c, h, s, s_ref, mx_ref)
        return carry

    lax.fori_loop(0, n_kc, logits_body, 0)
    _row_max(mx_ref, m_ref)

    def values_body(c, carry):
        k0 = pl.multiple_of(c * kc, kc)
        for g in range(N_KV):
            vt = _with_ones(kvt_ref[0, 2 * g + 1, :, pl.ds(k0, kc)])
            for r in range(GROUP):
                _values_step(c, g * GROUP + r, vt, s_ref, m_ref, acc_ref)
        return carry

    lax.fori_loop(0, n_kc, values_body, 0)
    _softmax_finish(o_ref, acc_ref)


def _dsa_prompt(pp, b, t, tq, kc):
    n_t = t // tq
    k_sel = min(IDX_TOPK, t // 4)
    kern = functools.partial(_dsa_prompt_kernel, k_sel=k_sel)
    return pl.pallas_call(
        kern, grid=(b, n_t),
        in_specs=[pl.BlockSpec((tq, WIDTH), lambda bb, i: (bb * n_t + i, 0)),
                  pl.BlockSpec((tq, WIDTH), lambda bb, i: (bb * n_t + i, 1)),
                  pl.BlockSpec((tq, LANES), lambda bb, i: (bb * n_t + i, 0)),
                  pl.BlockSpec((1, IDX_DIM, t), lambda bb, i: (bb, 0, 0)),
                  pl.BlockSpec((1, KV_SLABS, HEAD_DIM, t), lambda bb, i: (bb, 0, 0, 0))],
        out_specs=pl.BlockSpec((tq, WIDTH), lambda bb, i: (bb * n_t + i, 0)),
        out_shape=jax.ShapeDtypeStruct((b * t, WIDTH), BF16),
        scratch_shapes=[pltpu.VMEM((t // kc, tq, kc), F32), pltpu.VMEM((t // kc, kc, tq), F32),
                        pltpu.VMEM((tq, WIDTH), BF16),
                        pltpu.VMEM((t // kc, N_HEADS, tq, kc), F32), pltpu.VMEM((N_HEADS, tq, LANES), F32),
                        pltpu.VMEM((N_HEADS, tq, 1), F32), pltpu.VMEM((N_HEADS, tq, ACC_W), F32)],
        compiler_params=_cparams("arbitrary", "arbitrary"), name="dsa_prompt",
    )(pp["q16"], pp["q16"], pp["wi"], pp["kit16"], pp["kvat16"])


def _top_blocks(gate, valid, axis):
    idx = lax.broadcasted_iota(jnp.int32, gate.shape, axis).astype(F32)
    g = jnp.where(valid, gate, -jnp.inf)
    sel = jnp.zeros(gate.shape, F32)
    for _ in range(MOBA_TOPK):
        m = jnp.max(g, axis=axis, keepdims=True)
        first = jnp.min(jnp.where(g == m, idx, float(gate.shape[axis])), axis=axis, keepdims=True)
        pick = idx == first
        sel = jnp.where(jnp.logical_and(pick, m > -jnp.inf), 1.0, sel)
        g = jnp.where(pick, -jnp.inf, g)
    return sel


def _block_indicator(n_rows, n_cols, scale):
    blk = lax.broadcasted_iota(jnp.int32, (n_rows, n_cols), 1) // MOBA_BLOCK
    return jnp.where(blk == lax.broadcasted_iota(jnp.int32, (n_rows, n_cols), 0), scale, 0.0).astype(BF16)


def _moba_prompt_kernel(qb_ref, kvt_ref, o_ref, kext_ref, kmean_ref, qe_ref, s_ref, mx_ref, m_ref, acc_ref):
    own = pl.program_id(1)
    t = kvt_ref.shape[3]
    tq = qb_ref.shape[0]
    n_ext = kext_ref.shape[1] - HEAD_DIM

    @pl.when(own == 0)
    def _():
        ind = _block_indicator(n_ext, t, 1.0)
        for g in range(N_KV):
            kt = kvt_ref[0, 2 * g]
            kext_ref[g, :HEAD_DIM, :] = kt
            kext_ref[g, HEAD_DIM:, :] = ind
            kmean_ref[g] = lax.dot_general(ind, kt, _NT, preferred_element_type=F32) * (1.0 / MOBA_BLOCK)

    blk = lax.broadcasted_iota(jnp.int32, (n_ext, tq), 0)
    for h in range(N_HEADS):
        g = h // GROUP
        q = qb_ref[:, h * HEAD_DIM:(h + 1) * HEAD_DIM]
        gate_t = lax.dot_general(kmean_ref[g].astype(BF16), q, _NT, preferred_element_type=F32)
        picked_t = _top_blocks(gate_t, blk < own, 0)
        allowed = jnp.logical_or(picked_t > 0.5, blk == own)
        bias = _pad_rows(jnp.where(allowed, 0.0, MASKED), LANES).T[:, :n_ext]
        qe_ref[h] = jnp.concatenate([q * SM_SCALE, bias.astype(BF16)], axis=1)

    mx_ref[...] = jnp.full(mx_ref.shape, MASKED, F32)
    acc_ref[...] = jnp.zeros(acc_ref.shape, F32)

    def logits(n, extra_bias):
        k0 = pl.multiple_of(n * MOBA_BLOCK, MOBA_BLOCK)
        for g in range(N_KV):
            ke = kext_ref[g, :, pl.ds(k0, MOBA_BLOCK)]
            for r in range(GROUP):
                h = g * GROUP + r
                s = jnp.dot(qe_ref[h], ke, preferred_element_type=F32)
                _logits_step(n, h, s if extra_bias is None else s + extra_bias, s_ref, mx_ref)

    def logits_body(n, carry):
        logits(n, None)
        return carry

    lax.fori_loop(0, own, logits_body, 0)
    causal = (lax.broadcasted_iota(jnp.int32, (tq, MOBA_BLOCK), 1)
              <= lax.broadcasted_iota(jnp.int32, (tq, MOBA_BLOCK), 0))
    logits(own, jnp.where(causal, 0.0, MASKED))
    _row_max(mx_ref, m_ref)

    def values_body(n, carry):
        k0 = pl.multiple_of(n * MOBA_BLOCK, MOBA_BLOCK)
        for g in range(N_KV):
            vt = _with_ones(kvt_ref[0, 2 * g + 1, :, pl.ds(k0, MOBA_BLOCK)])
            for r in range(GROUP):
                _values_step(n, g * GROUP + r, vt, s_ref, m_ref, acc_ref)
        return carry

    lax.fori_loop(0, own + 1, values_body, 0)
    _softmax_finish(o_ref, acc_ref)


def _moba_prompt(pp, b, t):
    tq = MOBA_BLOCK
    n_t = t // tq
    assert t % MOBA_BLOCK == 0 and n_t <= LANES
    n_ext = -(-n_t // BF16_ROWS) * BF16_ROWS
    return pl.pallas_call(
        _moba_prompt_kernel, grid=(b, n_t),
        in_specs=[pl.BlockSpec((tq, WIDTH), lambda bb, i: (bb * n_t + i, 2)),
                  pl.BlockSpec((1, KV_SLABS, HEAD_DIM, t), lambda bb, i: (bb, 0, 0, 0))],
        out_specs=pl.BlockSpec((tq, WIDTH), lambda bb, i: (bb * n_t + i, 0)),
        out_shape=jax.ShapeDtypeStruct((b * t, WIDTH), BF16),
        scratch_shapes=[pltpu.VMEM((N_KV, HEAD_DIM + n_ext, t), BF16), pltpu.VMEM((N_KV, n_ext, HEAD_DIM), F32),
                        pltpu.VMEM((N_HEADS, tq, HEAD_DIM + n_ext), BF16),
                        pltpu.VMEM((n_t, N_HEADS, tq, MOBA_BLOCK), F32), pltpu.VMEM((N_HEADS, tq, LANES), F32),
                        pltpu.VMEM((N_HEADS, tq, 1), F32), pltpu.VMEM((N_HEADS, tq, ACC_W), F32)],
        compiler_params=_cparams("arbitrary", "arbitrary"), name="moba_prompt",
    )(pp["q16"], pp["kvbt16"])


SEQ_PER_STEP = 2
DECODE_CHUNK = 2048
SELECT_SEQ_PER_STEP = 8


def _page_copy(pt_ref, pool, buf, sems, j, seq, slot, u, p):
    off = pl.multiple_of(p * PAGE_SIZE, PAGE_SIZE)
    lead = (slice(None),) * (len(buf.shape) - 3)
    return pltpu.make_async_copy(pool.at[pt_ref[seq, p]], buf.at[(slot, u) + lead + (pl.ds(off, PAGE_SIZE),)],
                                 sems.at[2 * j + slot])


def _paged_fetch(pt_ref, pools, bufs, sems, past):
    b = pl.program_id(0)
    nb = pl.num_programs(0)
    slot = b % 2
    n_pages = past // PAGE_SIZE
    n_seq = bufs[0].shape[1]

    def each_copy(step, sl, p, fn):
        for u in range(n_seq):
            for j, (pool, buf) in enumerate(zip(pools, bufs)):
                fn(_page_copy(pt_ref, pool, buf, sems, j, step * n_seq + u, sl, u, p))

    def start(step, sl):
        def body(p, carry):
            each_copy(step, sl, p, lambda cp: cp.start())
            return carry
        lax.fori_loop(0, n_pages, body, 0)

    @pl.when(b == 0)
    def _():
        start(0, 0)

    @pl.when(b + 1 < nb)
    def _():
        start(b + 1, 1 - slot)

    def wait_body(p, carry):
        each_copy(b, slot, p, lambda cp: cp.wait())
        return carry
    lax.fori_loop(0, n_pages, wait_body, 0)
    return slot


def _stack_heads(q, heads, scale=1.0):
    qf = q.astype(F32) * scale
    return jnp.concatenate([qf[:, h * HEAD_DIM:(h + 1) * HEAD_DIM] for h in heads], axis=0).astype(BF16)


def _pad_rows(x, n):
    return jnp.concatenate([x, jnp.zeros((n - x.shape[0], x.shape[1]), x.dtype)], axis=0)


def _chunked_attention(queries, kt_of, vt_of, bias_of, bias_new, k_new, v_new, s_ref, chunk):
    n = len(queries)
    rows = queries[0].shape[0]
    n_chunks = s_ref.shape[2] // chunk

    def logits_body(c, mx):
        k0 = pl.multiple_of(c * chunk, chunk)
        out = []
        for i in range(n):
            s = jnp.dot(queries[i], kt_of(i, k0), preferred_element_type=F32) + bias_of(i, k0)
            s_ref[i, :, pl.ds(k0, chunk)] = s
            out.append(jnp.maximum(mx[i], _fold_lanes(s, jnp.maximum)))
        return tuple(out)

    mx = lax.fori_loop(0, n_chunks, logits_body, tuple(jnp.full((rows, LANES), MASKED, F32) for _ in range(n)))
    s_new = [lax.dot_general(queries[i], k_new[i], _NT, preferred_element_type=F32) + bias_new[i] for i in range(n)]
    m = [jnp.maximum(jnp.max(mx[i], axis=1, keepdims=True), jnp.max(s_new[i], axis=1, keepdims=True))
         for i in range(n)]

    def values_body(c, carry):
        k0 = pl.multiple_of(c * chunk, chunk)
        out = []
        for i in range(n):
            l, acc = carry[i]
            p = jnp.exp(s_ref[i, :, pl.ds(k0, chunk)] - m[i])
            out.append((l + _fold_lanes(p, jnp.add),
                        acc + lax.dot_general(p.astype(BF16), vt_of(i, k0), _NT, preferred_element_type=F32)))
        return tuple(out)

    init = tuple((jnp.zeros((rows, LANES), F32), jnp.zeros((rows, HEAD_DIM), F32)) for _ in range(n))
    carry = lax.fori_loop(0, n_chunks, values_body, init)
    outs = []
    for i in range(n):
        l, acc = carry[i]
        p_new = jnp.exp(s_new[i] - m[i])
        l = jnp.sum(l, axis=1, keepdims=True) + jnp.sum(p_new, axis=1, keepdims=True)
        acc = acc + jnp.dot(p_new.astype(BF16), v_new[i], preferred_element_type=F32)
        outs.append(acc / l)
    return outs


def _store_heads(o_ref, u, per_group, nq):
    outs = [per_group[g][r * nq:(r + 1) * nq, :] for g in range(N_KV) for r in range(GROUP)]
    o_ref[u] = jnp.concatenate(outs, axis=1).astype(o_ref.dtype)


def _new_kv(kv_new, g):
    k = kv_new[:, 2 * g * HEAD_DIM:(2 * g + 1) * HEAD_DIM]
    v = kv_new[:, (2 * g + 1) * HEAD_DIM:(2 * g + 2) * HEAD_DIM]
    return _pad_rows(k, LANES).astype(BF16), _pad_rows(v, LANES).astype(BF16)


def _units():
    return [(u, g) for u in range(SEQ_PER_STEP) for g in range(N_KV)]


def _dsa_select_kernel(pt_ref, q_ref, kiw_ref, idx_pool, bias_ref, idxbuf, score_ref, sems, *, past, k_sel, chunk):
    n_seq, nq = q_ref.shape[0], q_ref.shape[1]
    slot = _paged_fetch(pt_ref, (idx_pool,), (idxbuf,), sems, past)
    col_new = lax.broadcasted_iota(jnp.int32, (1, LANES), 1)
    qidx = lax.broadcasted_iota(jnp.int32, (nq, 1), 0)
    qi_rows = [_stack_heads(q_ref[u], range(IDX_HEADS)) for u in range(n_seq)]
    kiw = [kiw_ref[u] for u in range(n_seq)]

    def head_sum(s, u):
        score = jnp.zeros((nq, s.shape[1]), F32)
        for h in range(IDX_HEADS):
            score = score + jnp.maximum(s[h * nq:(h + 1) * nq, :], 0.0) * kiw[u][:, IDX_DIM + h:IDX_DIM + h + 1]
        return score

    def idx_body(c, carry):
        k0 = pl.multiple_of(c * chunk, chunk)
        for u in range(n_seq):
            keys = idxbuf[slot, u, :, pl.ds(k0, chunk)].astype(BF16)
            score_ref[0, u * nq:(u + 1) * nq, pl.ds(k0, chunk)] = head_sum(
                jnp.dot(qi_rows[u], keys, preferred_element_type=F32), u)
        return carry

    lax.fori_loop(0, past // chunk, idx_body, 0)
    for u in range(n_seq):
        k_new = _pad_rows(kiw[u][:, :IDX_DIM], LANES).astype(BF16)
        s = head_sum(lax.dot_general(qi_rows[u], k_new, _NT, preferred_element_type=F32), u)
        score_ref[0, u * nq:(u + 1) * nq, past:] = jnp.where(col_new <= qidx, s, -jnp.inf)

    n_valid = jnp.concatenate([past + qidx + 1] * n_seq, axis=0)
    thr, jthr = _select_threshold(score_ref, 1, n_valid, k_sel, 1)
    x = score_ref[0]
    col = lax.broadcasted_iota(jnp.int32, (1, x.shape[1]), 1).astype(F32)
    sel = jnp.logical_or(x > thr, jnp.logical_and(x == thr, col <= jthr))
    bias = jnp.where(sel, 0.0, MASKED)
    for u in range(n_seq):
        bias_ref[u] = bias[u * nq:(u + 1) * nq, :]


def _dsa_decode_kernel(pt_ref, q_ref, kva_ref, bias_ref, kv_pool, o_ref, kvbuf, s_ref, sems, *, past, chunk):
    nq = q_ref.shape[1]
    slot = _paged_fetch(pt_ref, (kv_pool,), (kvbuf,), sems, past)

    def bias_rows(u, k0, width):
        return jnp.concatenate([bias_ref[u, :, pl.ds(k0, width)]] * GROUP, axis=0)

    units = _units()
    queries = [_stack_heads(q_ref[u], range(g * GROUP, (g + 1) * GROUP), SM_SCALE) for u, g in units]
    new = [_new_kv(kva_ref[u], g) for u, g in units]
    outs = _chunked_attention(
        queries,
        lambda i, k0: kvbuf[slot, units[i][0], 2 * units[i][1], :, pl.ds(k0, chunk)].astype(BF16),
        lambda i, k0: kvbuf[slot, units[i][0], 2 * units[i][1] + 1, :, pl.ds(k0, chunk)].astype(BF16),
        lambda i, k0: bias_rows(units[i][0], k0, chunk),
        [bias_rows(u, past, LANES) for u, g in units], [kv[0] for kv in new], [kv[1] for kv in new],
        s_ref, chunk)
    for u in range(SEQ_PER_STEP):
        _store_heads(o_ref, u, outs[u * N_KV:(u + 1) * N_KV], nq)


def _moba_decode_kernel(pt_ref, q_ref, kvb_ref, kv_pool, o_ref, kvbuf, s_ref, sems, *, past, chunk):
    nq = q_ref.shape[1]
    n_blk = past // MOBA_BLOCK
    n_chunks = past // chunk
    slot = _paged_fetch(pt_ref, (kv_pool,), (kvbuf,), sems, past)
    rows = GROUP * nq
    units = _units()
    kt_of = lambda i, k0: kvbuf[slot, units[i][0], 2 * units[i][1], :, pl.ds(k0, chunk)].astype(BF16)

    def indicator(k0):
        blk = (k0 + lax.broadcasted_iota(jnp.int32, (n_blk, chunk), 1)) // MOBA_BLOCK
        return jnp.where(blk == lax.broadcasted_iota(jnp.int32, (n_blk, chunk), 0), 1.0, 0.0).astype(BF16)

    def mean_body(c, sums):
        k0 = pl.multiple_of(c * chunk, chunk)
        ind = indicator(k0)
        return tuple(sums[i] + lax.dot_general(ind, kt_of(i, k0), _NT, preferred_element_type=F32)
                     for i in range(len(units)))

    sums = lax.fori_loop(0, n_chunks, mean_body, tuple(jnp.zeros((n_blk, HEAD_DIM), F32) for _ in units))

    queries, bias_blk = [], []
    for i, (u, g) in enumerate(units):
        qb = q_ref[u]
        kmean = (sums[i] * (1.0 / MOBA_BLOCK)).astype(BF16)
        gate = lax.dot_general(_stack_heads(qb, range(g * GROUP, (g + 1) * GROUP)), kmean, _NT,
                               preferred_element_type=F32)
        picked = _top_blocks(gate, jnp.full(gate.shape, True), 1)
        bias_blk.append(jnp.where(picked > 0.5, 0.0, MASKED).astype(BF16))
        queries.append(_stack_heads(qb, range(g * GROUP, (g + 1) * GROUP), SM_SCALE))

    col = lax.broadcasted_iota(jnp.int32, (rows, LANES), 1)
    qidx = lax.broadcasted_iota(jnp.int32, (rows, LANES), 0) % nq
    bias_new = jnp.where(col <= qidx, 0.0, MASKED)
    new = [_new_kv(kvb_ref[u], g) for u, g in units]
    outs = _chunked_attention(
        queries, kt_of,
        lambda i, k0: kvbuf[slot, units[i][0], 2 * units[i][1] + 1, :, pl.ds(k0, chunk)].astype(BF16),
        lambda i, k0: jnp.dot(bias_blk[i], indicator(k0), preferred_element_type=F32),
        [bias_new] * len(units), [kv[0] for kv in new], [kv[1] for kv in new], s_ref, chunk)
    for u in range(SEQ_PER_STEP):
        _store_heads(o_ref, u, outs[u * N_KV:(u + 1) * N_KV], nq)


def _sample_specs(n_seq, nq, widths_blocks):
    return [pl.BlockSpec((n_seq, nq, w), lambda bb, pt, j=j: (bb, 0, j)) for w, j in widths_blocks]


def _dsa_decode(ps, page_table, idx_pool_t, kv_pool_t, db, nq):
    past = page_table.shape[1] * PAGE_SIZE
    lp = past + LANES
    k_sel = min(IDX_TOPK, (past + nq) // 4)
    chunk = math.gcd(past, DECODE_CHUNK)
    q16 = ps["q16"].reshape(db, nq, -1)
    n_sel = math.gcd(db, SELECT_SEQ_PER_STEP)
    assert db % SEQ_PER_STEP == 0

    bias = pl.pallas_call(
        functools.partial(_dsa_select_kernel, past=past, k_sel=k_sel, chunk=chunk),
        grid_spec=pltpu.PrefetchScalarGridSpec(
            num_scalar_prefetch=1, grid=(db // n_sel,),
            in_specs=_sample_specs(n_sel, nq, [(WIDTH, 1), (LANES, 0)]) + [pl.BlockSpec(memory_space=pl.ANY)],
            out_specs=_sample_specs(n_sel, nq, [(lp, 0)])[0],
            scratch_shapes=[pltpu.VMEM((2, n_sel, IDX_DIM, past), F32), pltpu.VMEM((1, n_sel * nq, lp), F32),
                            pltpu.SemaphoreType.DMA((2,))]),
        out_shape=jax.ShapeDtypeStruct((db, nq, lp), F32),
        compiler_params=_cparams("arbitrary"), name="dsa_select",
    )(page_table, q16, ps["kiw"].reshape(db, nq, LANES), idx_pool_t)

    return pl.pallas_call(
        functools.partial(_dsa_decode_kernel, past=past, chunk=chunk),
        grid_spec=pltpu.PrefetchScalarGridSpec(
            num_scalar_prefetch=1, grid=(db // SEQ_PER_STEP,),
            in_specs=_sample_specs(SEQ_PER_STEP, nq, [(WIDTH, 0), (KV_WIDTH, 0), (lp, 0)])
            + [pl.BlockSpec(memory_space=pl.ANY)],
            out_specs=_sample_specs(SEQ_PER_STEP, nq, [(WIDTH, 0)])[0],
            scratch_shapes=[pltpu.VMEM((2, SEQ_PER_STEP, KV_SLABS, HEAD_DIM, past), F32),
                            pltpu.VMEM((SEQ_PER_STEP * N_KV, GROUP * nq, past), F32),
                            pltpu.SemaphoreType.DMA((2,))]),
        out_shape=jax.ShapeDtypeStruct((db, nq, WIDTH), BF16),
        compiler_params=_cparams("arbitrary"), name="dsa_decode",
    )(page_table, q16, ps["kva"].reshape(db, nq, KV_WIDTH), bias, kv_pool_t)


def _moba_decode(ps, page_table, kv_pool_t, db, nq):
    past = page_table.shape[1] * PAGE_SIZE
    assert past % MOBA_BLOCK == 0 and nq <= MOBA_BLOCK and db % SEQ_PER_STEP == 0
    qw = ps["q16"].shape[1]
    kern = functools.partial(_moba_decode_kernel, past=past, chunk=math.gcd(past, DECODE_CHUNK))
    grid_spec = pltpu.PrefetchScalarGridSpec(
        num_scalar_prefetch=1, grid=(db // SEQ_PER_STEP,),
        in_specs=_sample_specs(SEQ_PER_STEP, nq, [(WIDTH, 2), (KV_WIDTH, 0)]) + [pl.BlockSpec(memory_space=pl.ANY)],
        out_specs=_sample_specs(SEQ_PER_STEP, nq, [(WIDTH, 0)])[0],
        scratch_shapes=[pltpu.VMEM((2, SEQ_PER_STEP, KV_SLABS, HEAD_DIM, past), F32),
                        pltpu.VMEM((SEQ_PER_STEP * N_KV, GROUP * nq, past), F32), pltpu.SemaphoreType.DMA((2,))])
    return pl.pallas_call(
        kern, grid_spec=grid_spec, out_shape=jax.ShapeDtypeStruct((db, nq, WIDTH), BF16),
        compiler_params=_cparams("arbitrary"), name="moba_decode",
    )(page_table, ps["q16"].reshape(db, nq, qw), ps["kvb"].reshape(db, nq, KV_WIDTH), kv_pool_t)


def _out_kernel(x_ref, oa_ref, ob_ref, gate_ref, wba_ref, wbb_ref, wo_ref, y_ref):
    d = x_ref.shape[1]
    ua = oa_ref[...] * gate_ref[:, 0:WIDTH]
    ub = ob_ref[...] * gate_ref[:, WIDTH:2 * WIDTH]
    a = jnp.dot(ua, wba_ref[...], preferred_element_type=F32)
    b = jnp.dot(ub, wbb_ref[...], preferred_element_type=F32)
    ga = gate_ref[:, 2 * WIDTH:2 * WIDTH + d].astype(F32)
    gb = gate_ref[:, 2 * WIDTH + d:2 * WIDTH + 2 * d].astype(F32)
    merged = (ga * a + gb * b).astype(BF16)
    y_ref[...] = x_ref[...] + jnp.dot(merged, wo_ref[...], preferred_element_type=F32)


def _out_proj(x2d, oa, ob, gate16, wts, tm):
    n, d = x2d.shape
    row = lambda w: pl.BlockSpec((tm, w), lambda i: (i, 0))
    return pl.pallas_call(
        _out_kernel, grid=(n // tm,),
        in_specs=[row(d), row(WIDTH), row(WIDTH), row(gate16.shape[1]),
                  _const_spec((WIDTH, d)), _const_spec((WIDTH, d)), _const_spec((d, d))],
        out_specs=row(d), out_shape=jax.ShapeDtypeStruct((n, d), F32),
        compiler_params=_cparams("arbitrary"), name="out_proj",
    )(x2d, oa, ob, gate16, wts["wba"], wts["wbb"], wts["wo"])


def _rope_angles(pos):
    inv = ROPE_THETA ** (-jnp.arange(ROT_HALF, dtype=F32) / ROT_HALF)
    ang = pos.astype(F32)[:, None] * inv[None, :]
    return jnp.cos(ang), jnp.sin(ang)


def _rope_tables(pos):
    p = pos.shape[0]
    c, s = _rope_angles(pos)
    rest = HEAD_DIM - 2 * ROT_HALF
    z8, zr = jnp.zeros((p, ROT_HALF), F32), jnp.zeros((p, rest), F32)
    c_head = jnp.concatenate([c, c, jnp.ones((p, rest), F32)], axis=1)
    s1_head = jnp.concatenate([-s, z8, zr], axis=1)
    s2_head = jnp.concatenate([z8, s, zr], axis=1)
    one, zero = jnp.ones((p, HEAD_DIM), F32), jnp.zeros((p, HEAD_DIM), F32)
    cat = lambda a, b: jnp.concatenate([a, b], axis=1)
    return (cat(c_head, c_head), cat(s1_head, s1_head), cat(s2_head, s2_head),
            cat(c_head, one), cat(s1_head, zero), cat(s2_head, zero))


def _prep_weights(norm_g, w_in, qn_a, kn_a, qn_i, kn_i, qn_b, kn_b, w_ba, w_bb, w_out):
    d = w_in.shape[0]
    kvw = N_KV * HEAD_DIM
    splits = (WIDTH, kvw, kvw, IDX_HEADS * IDX_DIM, IDX_DIM, IDX_HEADS, WIDTH, WIDTH, kvw, kvw, WIDTH, d, d)
    offs = np.concatenate([[0], np.cumsum(splits)])
    qa, ka, va, qi, ki, wi, za, qb, kb, vb, zb, ga, gb = [w_in[:, offs[j]:offs[j + 1]] for j in range(13)]
    hd = HEAD_DIM
    pad = jnp.zeros((d, LANES - IDX_DIM - IDX_HEADS), F32)
    slabs = jnp.concatenate([ka[:, :hd], va[:, :hd], ka[:, hd:], va[:, hd:],
                             kb[:, :hd], vb[:, :hd], kb[:, hd:], vb[:, hd:], ki], axis=1)
    wk = jnp.concatenate([slabs, wi, pad], axis=1)
    wwi = jnp.concatenate([wi, jnp.zeros((d, LANES - IDX_HEADS), F32)], axis=1)
    zero = jnp.zeros((hd,), F32)
    one = jnp.ones((hd,), F32)
    wi_gain = jnp.concatenate([jnp.full((IDX_HEADS,), (IDX_HEADS * IDX_DIM) ** -0.5, F32),
                               jnp.zeros((hd - IDX_HEADS,), F32)])
    gka = jnp.concatenate([kn_a, zero, kn_a, zero, kn_b, zero, kn_b, zero, kn_i, zero])[None, :]
    gkb = jnp.concatenate([zero, one, zero, one, zero, one, zero, one, zero, wi_gain])[None, :]
    gkt = jnp.concatenate([kn_a, one, kn_a, one, kn_b, one, kn_b, one, kn_i])[:, None]
    gq = jnp.concatenate([jnp.tile(qn_a, N_HEADS), jnp.tile(qn_i, IDX_HEADS), jnp.tile(qn_b, N_HEADS)])[None, :]
    blk = np.arange(MXU_DIM) // HEAD_DIM
    bd = jnp.asarray((blk[:, None] == blk[None, :]) / HEAD_DIM, BF16)
    return dict(
        ng=norm_g[None, :].astype(F32),
        wq=jnp.concatenate([qa, qi, qb], axis=1).astype(BF16), wk=wk.astype(BF16),
        wkt=slabs.T.astype(BF16), wwi=wwi.astype(BF16),
        wg=jnp.concatenate([za, zb, ga, gb], axis=1).astype(BF16),
        gq=gq, gka=gka, gkb=gkb, gkt=gkt, bd=bd,
        wba=w_ba.astype(BF16), wbb=w_bb.astype(BF16), wo=w_out.astype(BF16))


def kernel(x_prompt, x_sample, cache_kv_a, cache_idx_k, cache_kv_b, page_table, norm_g, w_in, q_norm_a, k_norm_a,
           idx_q_norm, idx_k_norm, q_norm_b, k_norm_b, w_branch_a, w_branch_b, w_out):
    b, t, d = x_prompt.shape
    db, nq, _ = x_sample.shape
    depth = w_in.shape[0]
    n_phys = cache_kv_a.shape[1]
    past = page_table.shape[1] * PAGE_SIZE
    tm = 256
    tm_s = min(tm, db * nq)
    assert t % tm == 0 and tm_s % nq == 0 and (db * nq) % tm_s == 0

    pos_p = jnp.arange(t, dtype=jnp.int32)
    cos_p, sin_p = _rope_angles(pos_p)
    tab_p = _rope_tables(pos_p)[:3] + (cos_p.T, sin_p.T)
    tab_s = tuple(jnp.tile(a, (tm_s // nq, 1)) for a in _rope_tables(past + jnp.arange(nq, dtype=jnp.int32)))

    kv_t = lambda pool: jnp.transpose(pool, (0, 2, 3, 4, 1)).reshape(n_phys, KV_SLABS, HEAD_DIM, PAGE_SIZE)

    hp = x_prompt.reshape(b * t, d)
    hs = x_sample.reshape(db * nq, d)
    new = [[] for _ in range(6)]
    for l in range(depth):
        wts = _prep_weights(norm_g[l], w_in[l], q_norm_a[l], k_norm_a[l], idx_q_norm[l], idx_k_norm[l],
                            q_norm_b[l], k_norm_b[l], w_branch_a[l], w_branch_b[l], w_out[l])
        pp = _project_prompt(hp, tab_p, wts, b, t, tm)
        oa = _dsa_prompt(pp, b, t, tq=256, kc=min(512, t))
        ob = _moba_prompt(pp, b, t)
        hp = _out_proj(hp, oa, ob, pp["gate16"], wts, tm)

        ps = _project_sample(hs, tab_s, wts, tm_s)
        oa_s = _dsa_decode(ps, page_table, jnp.transpose(cache_idx_k[l], (0, 2, 1)), kv_t(cache_kv_a[l]), db, nq)
        ob_s = _moba_decode(ps, page_table, kv_t(cache_kv_b[l]), db, nq)
        hs = _out_proj(hs, oa_s.reshape(db * nq, WIDTH), ob_s.reshape(db * nq, WIDTH), ps["gate16"], wts, tm_s)

        to_tokens = lambda a: jnp.transpose(a.reshape(b, N_KV, 2, HEAD_DIM, t), (0, 4, 1, 2, 3))
        new[0].append(to_tokens(pp["kvat"]))
        new[1].append(jnp.transpose(pp["kit"], (0, 2, 1)))
        new[2].append(to_tokens(pp["kvbt"]))
        new[3].append(ps["kva"].reshape(db, nq, N_KV, 2, HEAD_DIM))
        new[4].append(ps["kiw"][:, :IDX_DIM].reshape(db, nq, IDX_DIM))
        new[5].append(ps["kvb"].reshape(db, nq, N_KV, 2, HEAD_DIM))
    return (hp.reshape(b, t, d), hs.reshape(db, nq, d)) + tuple(jnp.stack(a, axis=0) for a in new)
```

```python
import functools
import math

import numpy as np
import jax
import jax.numpy as jnp
from jax import lax
from jax.experimental import pallas as pl
from jax.experimental.pallas import tpu as pltpu

F32 = jnp.float32
BF16 = jnp.bfloat16

HEAD_DIM = 64
N_HEADS = 8
N_KV = 2
GROUP = N_HEADS // N_KV
WIDTH = N_HEADS * HEAD_DIM
KV_WIDTH = N_KV * 2 * HEAD_DIM
KV_SLABS = N_KV * 2
IDX_HEADS = 8
IDX_DIM = 64
IDX_TOPK = 256
MOBA_BLOCK = 256
MOBA_TOPK = 3
PAGE_SIZE = 128
ROPE_THETA = 500000.0
ROT_HALF = HEAD_DIM // 4 // 2
RMS_EPS = 1e-6
SM_SCALE = HEAD_DIM ** -0.5

LANES = 128
SUBLANES = 8
MXU_DIM = 256
BF16_ROWS = 16
VMEM_LIMIT_BYTES = 56 * 1024 * 1024

MASKED = -1e30
BISECT_STEPS = 6
FINE_STEPS = 2

_NT = (((1,), (1,)), ((), ()))


def _cparams(*sem):
    return pltpu.CompilerParams(dimension_semantics=sem, vmem_limit_bytes=VMEM_LIMIT_BYTES)


def _normed_input(x_ref, ng_ref):
    x = x_ref[...]
    inv = lax.rsqrt(jnp.mean(x * x, axis=-1, keepdims=True) + RMS_EPS)
    return (x * inv * ng_ref[...]).astype(BF16)


def _head_mean_sq(z, bd_ref):
    parts = []
    for c in range(0, z.shape[1], MXU_DIM):
        cw = min(MXU_DIM, z.shape[1] - c)
        zc = z[:, c:c + cw]
        parts.append(jnp.dot((zc * zc).astype(BF16), bd_ref[:cw, :cw], preferred_element_type=F32))
    return parts[0] if len(parts) == 1 else jnp.concatenate(parts, axis=1)


def _rope_lanes(y, c_ref, s1_ref, s2_ref):
    return (y * c_ref[...] + pltpu.roll(y, LANES - ROT_HALF, 1) * s1_ref[...]
            + pltpu.roll(y, ROT_HALF, 1) * s2_ref[...])


def _queries_and_gates(h, wq_ref, wg_ref, gq_ref, bd_ref, cq_ref, s1q_ref, s2q_ref, q16_ref, gate16_ref):
    for c0 in range(0, q16_ref.shape[1], 512):
        z = jnp.dot(h, wq_ref[:, c0:c0 + 512], preferred_element_type=F32)
        y = z * lax.rsqrt(_head_mean_sq(z, bd_ref) + RMS_EPS) * gq_ref[:, c0:c0 + 512]
        for j in range(0, 512, LANES):
            q16_ref[:, c0 + j:c0 + j + LANES] = _rope_lanes(y[:, j:j + LANES], cq_ref, s1q_ref, s2q_ref).astype(BF16)
    n_silu = 2 * WIDTH
    for c0 in range(0, gate16_ref.shape[1], 512):
        z = jnp.dot(h, wg_ref[:, c0:c0 + 512], preferred_element_type=F32)
        sig = 1.0 / (1.0 + jnp.exp(-z))
        gate16_ref[:, c0:c0 + 512] = (z * sig if c0 < n_silu else sig).astype(BF16)


def _proj_sample_kernel(x_ref, ng_ref, wq_ref, wk_ref, wg_ref, gq_ref, gka_ref, gkb_ref, bd_ref,
                        cq_ref, s1q_ref, s2q_ref, ck_ref, s1k_ref, s2k_ref,
                        q16_ref, gate16_ref, kva_ref, kvb_ref, kiw_ref):
    h = _normed_input(x_ref, ng_ref)
    _queries_and_gates(h, wq_ref, wg_ref, gq_ref, bd_ref, cq_ref, s1q_ref, s2q_ref, q16_ref, gate16_ref)
    z = jnp.dot(h, wk_ref[...], preferred_element_type=F32)
    y = z * (lax.rsqrt(_head_mean_sq(z, bd_ref) + RMS_EPS) * gka_ref[...] + gkb_ref[...])
    r = [_rope_lanes(y[:, j * LANES:(j + 1) * LANES], ck_ref, s1k_ref, s2k_ref) for j in range(5)]
    for j in range(2):
        kva_ref[:, j * LANES:(j + 1) * LANES] = r[j]
        kvb_ref[:, j * LANES:(j + 1) * LANES] = r[2 + j]
    kiw_ref[...] = r[4]


def _proj_prompt_kernel(x_ref, ng_ref, wq_ref, wkt_ref, wwi_ref, wg_ref, gq_ref, gkt_ref, bd_ref,
                        cq_ref, s1q_ref, s2q_ref, ct_ref, st_ref,
                        q16_ref, gate16_ref, wi_ref, kvat_ref, kvbt_ref, kit_ref, kvat16_ref, kvbt16_ref, kit16_ref):
    h = _normed_input(x_ref, ng_ref)
    _queries_and_gates(h, wq_ref, wg_ref, gq_ref, bd_ref, cq_ref, s1q_ref, s2q_ref, q16_ref, gate16_ref)
    wi_ref[...] = jnp.dot(h, wwi_ref[...], preferred_element_type=F32) * (IDX_HEADS * IDX_DIM) ** -0.5

    zt = lax.dot_general(wkt_ref[...], h, _NT, preferred_element_type=F32)
    cos, sin = ct_ref[...], st_ref[...]

    def key_head(j):
        z = zt[j * HEAD_DIM:(j + 1) * HEAD_DIM, :]
        y = z * lax.rsqrt(jnp.mean(z * z, axis=0, keepdims=True) + RMS_EPS) * gkt_ref[j * HEAD_DIM:(j + 1) * HEAD_DIM, :]
        y1, y2 = y[:ROT_HALF], y[ROT_HALF:2 * ROT_HALF]
        return jnp.concatenate([y1 * cos - y2 * sin, y2 * cos + y1 * sin, y[2 * ROT_HALF:]], axis=0)

    for j in range(KV_SLABS):
        for base, out_ref, out16_ref in ((0, kvat_ref, kvat16_ref), (KV_SLABS, kvbt_ref, kvbt16_ref)):
            slab = key_head(base + j) if j % 2 == 0 else zt[(base + j) * HEAD_DIM:(base + j + 1) * HEAD_DIM, :]
            out_ref[0, j] = slab
            out16_ref[0, j] = slab.astype(BF16)
    ki = key_head(2 * KV_SLABS)
    kit_ref[0] = ki
    kit16_ref[0] = ki.astype(BF16)


def _const_spec(shape, n_grid=1):
    if n_grid == 1:
        return pl.BlockSpec(shape, lambda i: (0,) * len(shape))
    return pl.BlockSpec(shape, lambda b, i: (0,) * len(shape))


def _project_sample(x2d, tables, wts, tm):
    n, d = x2d.shape
    wq, wk, wg = wts["wq"], wts["wk"], wts["wg"]
    row = lambda w: pl.BlockSpec((tm, w), lambda i: (i, 0))
    tab = pl.BlockSpec((tm, LANES), lambda i: (0, 0))
    in_specs = [row(d), _const_spec((1, d)), _const_spec(wq.shape), _const_spec(wk.shape), _const_spec(wg.shape),
                _const_spec((1, wq.shape[1])), _const_spec((1, wk.shape[1])), _const_spec((1, wk.shape[1])),
                _const_spec((MXU_DIM, MXU_DIM))] + [tab] * 6
    out_shape = [
        jax.ShapeDtypeStruct((n, wq.shape[1]), BF16),
        jax.ShapeDtypeStruct((n, wg.shape[1]), BF16),
        jax.ShapeDtypeStruct((n, KV_WIDTH), F32),
        jax.ShapeDtypeStruct((n, KV_WIDTH), F32),
        jax.ShapeDtypeStruct((n, LANES), F32),
    ]
    outs = pl.pallas_call(
        _proj_sample_kernel, grid=(n // tm,), in_specs=in_specs, out_specs=[row(s.shape[1]) for s in out_shape],
        out_shape=out_shape, compiler_params=_cparams("arbitrary"), name="project_sample",
    )(x2d, wts["ng"], wq, wk, wg, wts["gq"], wts["gka"], wts["gkb"], wts["bd"], *tables)
    return dict(zip(("q16", "gate16", "kva", "kvb", "kiw"), outs))


def _project_prompt(x2d, tables, wts, b, t, tm):
    n, d = x2d.shape
    n_t = t // tm
    wq, wkt, wwi, wg = wts["wq"], wts["wkt"], wts["wwi"], wts["wg"]
    row = lambda w: pl.BlockSpec((tm, w), lambda bb, i: (bb * n_t + i, 0))
    tab = pl.BlockSpec((tm, LANES), lambda bb, i: (i, 0))
    tab_t = pl.BlockSpec((ROT_HALF, tm), lambda bb, i: (0, i))
    cs = lambda shape: _const_spec(shape, 2)
    in_specs = [row(d), cs((1, d)), cs(wq.shape), cs(wkt.shape), cs(wwi.shape), cs(wg.shape),
                cs((1, wq.shape[1])), cs((wkt.shape[0], 1)), cs((MXU_DIM, MXU_DIM)), tab, tab, tab, tab_t, tab_t]
    kv_t = lambda dt: jax.ShapeDtypeStruct((b, KV_SLABS, HEAD_DIM, t), dt)
    ki_t = lambda dt: jax.ShapeDtypeStruct((b, IDX_DIM, t), dt)
    out_shape = [jax.ShapeDtypeStruct((n, wq.shape[1]), BF16), jax.ShapeDtypeStruct((n, wg.shape[1]), BF16),
                 jax.ShapeDtypeStruct((n, LANES), F32),
                 kv_t(F32), kv_t(F32), ki_t(F32), kv_t(BF16), kv_t(BF16), ki_t(BF16)]
    kv_spec = pl.BlockSpec((1, KV_SLABS, HEAD_DIM, tm), lambda bb, i: (bb, 0, 0, i))
    ki_spec = pl.BlockSpec((1, IDX_DIM, tm), lambda bb, i: (bb, 0, i))
    out_specs = [row(wq.shape[1]), row(wg.shape[1]), row(LANES), kv_spec, kv_spec, ki_spec, kv_spec, kv_spec, ki_spec]
    outs = pl.pallas_call(
        _proj_prompt_kernel, grid=(b, n_t), in_specs=in_specs, out_specs=out_specs, out_shape=out_shape,
        compiler_params=_cparams("arbitrary", "arbitrary"), name="project_prompt",
    )(x2d, wts["ng"], wq, wkt, wwi, wg, wts["gq"], wts["gkt"], wts["bd"], *tables)
    return dict(zip(("q16", "gate16", "wi", "kvat", "kvbt", "kit", "kvat16", "kvbt16", "kit16"), outs))


def _fold_lanes(x, op):
    acc = x[:, :LANES]
    for j in range(1, x.shape[1] // LANES):
        acc = op(acc, x[:, j * LANES:(j + 1) * LANES])
    return acc


def _fold_groups(x, op, group):
    n = x.shape[0] // group
    parts = [x[j * group:(j + 1) * group, :] for j in range(min(n, SUBLANES))]
    for j in range(len(parts), n):
        parts[j % SUBLANES] = op(parts[j % SUBLANES], x[j * group:(j + 1) * group, :])
    while len(parts) > 1:
        parts = [op(parts[j], parts[j + 1]) for j in range(0, len(parts) - 1, 2)] + parts[len(parts) & ~1:]
    return parts[0]


def _select_threshold(score_ref, coarse_ref, n_chunks, n_valid, k, key_axis):
    width = score_ref.shape[1 + key_axis]
    rows = score_ref.shape[2 - key_axis]
    kf = float(k)
    neg, pos = -jnp.inf, jnp.inf

    def reducer(ref, group):
        def fold(x, op):
            if key_axis == 1:
                return _fold_lanes(x, op)
            return _fold_groups(x, op, group)
        acc_shape = (rows, LANES) if key_axis == 1 else (group, rows)

        def reduce_all(fn, op, init, final_reduce):
            def body(c, acc):
                return op(acc, fold(fn(ref[c], c), op))
            acc = lax.fori_loop(0, n_chunks, body, jnp.full(acc_shape, init, ref.dtype))
            return final_reduce(acc.astype(F32), axis=key_axis, keepdims=True)
        return reduce_all

    fine = reducer(score_ref, SUBLANES)
    coarse = reducer(coarse_ref, BF16_ROWS)
    one_b, zero_b = jnp.ones((), BF16), jnp.zeros((), BF16)
    neg_b, pos_b = jnp.full((), neg, BF16), jnp.full((), pos, BF16)

    def count(pred):
        return fine(lambda x, c: jnp.where(pred(x, c), 1.0, 0.0), jnp.add, 0.0, jnp.sum)

    def count_b(pred):
        return coarse(lambda x, c: jnp.where(pred(x, c), one_b, zero_b), jnp.add, 0.0, jnp.sum)

    class Fine:
        grid = staticmethod(lambda v: v)
        count_ge = staticmethod(lambda v: count(lambda x, c: x >= v))
        max_below = staticmethod(lambda v: fine(lambda x, c: jnp.where(x < v, x, neg), jnp.maximum, neg, jnp.max))
        min_from = staticmethod(lambda v: fine(lambda x, c: jnp.where(x >= v, x, pos), jnp.minimum, pos, jnp.min))

    class Coarse:
        grid = staticmethod(lambda v: v.astype(BF16).astype(F32))
        count_ge = staticmethod(lambda v: count_b(lambda x, c: x >= v.astype(BF16)))
        max_below = staticmethod(
            lambda v: coarse(lambda x, c: jnp.where(x < v.astype(BF16), x, neg_b), jnp.maximum, neg, jnp.max))
        min_from = staticmethod(
            lambda v: coarse(lambda x, c: jnp.where(x >= v.astype(BF16), x, pos_b), jnp.minimum, pos, jnp.min))

    small = n_valid <= k
    col = lambda v: jnp.full(n_valid.shape, v, F32)

    def finished(c_lo, c_hi, done):
        return jnp.where(jnp.logical_or(c_hi == kf - 1.0, c_lo == kf), 1.0, done)

    def kth_largest(ops, lo, hi, c_lo, c_hi, steps):
        def cond(s):
            return jnp.sum(1.0 - s[5]) > 0.0

        def body(s):
            lo, hi, c_lo, c_hi, snap, done = s
            vmax = ops.max_below(hi)
            c = ops.count_ge(vmax)
            live = done < 0.5
            hit = jnp.logical_and(live, c >= kf)
            move = jnp.logical_and(live, c < kf)
            snap = jnp.where(hit, 1.0, snap)
            hi = jnp.where(move, vmax, hi)
            c_hi = jnp.where(move, c, c_hi)
            done = finished(c_lo, c_hi, jnp.where(hit, 1.0, done))
            for _ in range(steps):
                mid = ops.grid(0.5 * lo + 0.5 * hi)
                c = ops.count_ge(mid)
                live = done < 0.5
                up = jnp.logical_and(live, c >= kf)
                down = jnp.logical_and(live, c < kf)
                lo = jnp.where(up, mid, lo)
                c_lo = jnp.where(up, c, c_lo)
                hi = jnp.where(down, mid, hi)
                c_hi = jnp.where(down, c, c_hi)
                done = finished(c_lo, c_hi, done)
            return lo, hi, c_lo, c_hi, snap, done

        done0 = finished(c_lo, c_hi, jnp.where(small, 1.0, 0.0).astype(F32))
        lo, hi, c_lo, c_hi, snap, _ = lax.while_loop(cond, body, (lo, hi, c_lo, c_hi, col(0.0), done0))
        from_hi = jnp.logical_or(snap > 0.5, c_hi == kf - 1.0)
        return jnp.where(from_hi, ops.max_below(hi), ops.min_from(lo))

    row_min = coarse(lambda x, c: jnp.where(x > neg_b, x, pos_b), jnp.minimum, pos, jnp.min)
    rb = kth_largest(Coarse, row_min, col(pos), n_valid.astype(F32), col(0.0), BISECT_STEPS)
    rbb = rb.astype(BF16)
    c_lo = count_b(lambda x, c: x >= rbb)
    c_hi = count_b(lambda x, c: x > rbb)
    lo = fine(lambda x, c: jnp.where(coarse_ref[c].astype(F32) == rb, x, pos), jnp.minimum, pos, jnp.min)
    hi = fine(lambda x, c: jnp.where(coarse_ref[c].astype(F32) > rb, x, pos), jnp.minimum, pos, jnp.min)
    thr = jnp.where(small, neg, kth_largest(Fine, lo, hi, c_lo, c_hi, FINE_STEPS))

    need = kf - count(lambda x, c: x > thr)
    n_eq = count(lambda x, c: x == thr)
    tie = jnp.logical_and(jnp.logical_not(small), n_eq > need)
    total = score_ref.shape[0] * width
    col0 = lax.broadcasted_iota(jnp.int32, (1, width) if key_axis == 1 else (width, 1), key_axis)

    def tie_search():
        lo = col(-1.0)
        hi = col(float(total - 1))
        for _ in range(int(np.ceil(np.log2(total))) + 1):
            mid = jnp.floor(0.5 * (lo + hi))
            ok = count(lambda x, c: jnp.logical_and(x == thr, (col0 + c * width).astype(F32) <= mid)) >= need
            hi = jnp.where(ok, mid, hi)
            lo = jnp.where(ok, lo, mid)
        return hi

    any_tie = jnp.sum(jnp.where(tie, 1.0, 0.0)) > 0.0
    jthr = lax.cond(any_tie, tie_search, lambda: col(float(total)))
    jthr = jnp.where(small, -1.0, jthr)
    return thr, jthr


ACC_W = HEAD_DIM + BF16_ROWS


def _with_ones(vt):
    return jnp.concatenate([vt, jnp.ones((BF16_ROWS, vt.shape[1]), vt.dtype)], axis=0)


def _logits_step(c, h, s, s_ref, mx_ref):
    s_ref[c, h] = s
    mx_ref[h] = jnp.maximum(mx_ref[h], _fold_lanes(s, jnp.maximum))


def _row_max(mx_ref, m_ref):
    for h in range(N_HEADS):
        m_ref[h] = jnp.max(mx_ref[h], axis=1, keepdims=True)


def _values_step(c, h, vt_ones, s_ref, m_ref, acc_ref):
    p = jnp.exp(s_ref[c, h] - m_ref[h])
    acc_ref[h] += lax.dot_general(p.astype(BF16), vt_ones, _NT, preferred_element_type=F32)


def _softmax_finish(o_ref, acc_ref):
    outs = []
    for h in range(N_HEADS):
        acc = acc_ref[h]
        outs.append(acc[:, :HEAD_DIM] / acc[:, HEAD_DIM:HEAD_DIM + 1])
    o_ref[...] = jnp.concatenate(outs, axis=1).astype(o_ref.dtype)


def _dsa_prompt_kernel(qa_ref, qi_ref, wi_ref, kit_ref, kvt_ref, o_ref,
                       score_ref, scoret_ref, coarse_ref, qs_ref, s_ref, mx_ref, m_ref, acc_ref, *, k_sel):
    i = pl.program_id(1)
    _, tq, kc = score_ref.shape
    n_kc = (i * tq + tq + kc - 1) // kc
    rows = i * tq + lax.broadcasted_iota(jnp.int32, (tq, 1), 0)
    w = wi_ref[:, :IDX_HEADS]
    col0 = lax.broadcasted_iota(jnp.int32, (1, kc), 1)

    def idx_body(c, carry):
        k0 = pl.multiple_of(c * kc, kc)
        kblk = kit_ref[0, :, pl.ds(k0, kc)]
        acc = jnp.zeros((tq, kc), F32)
        for h in range(IDX_HEADS):
            s = jnp.dot(qi_ref[:, h * IDX_DIM:(h + 1) * IDX_DIM], kblk, preferred_element_type=F32)
            acc = acc + jnp.maximum(s, 0.0) * w[:, h:h + 1]
        masked = jnp.where(col0 + k0 <= rows, acc, -jnp.inf)
        score_ref[c] = masked
        masked_t = masked.T
        scoret_ref[c] = masked_t
        coarse_ref[c] = masked_t.astype(BF16)
        return carry

    lax.fori_loop(0, n_kc, idx_body, 0)
    n_valid = i * tq + 1 + lax.broadcasted_iota(jnp.int32, (1, tq), 1)
    thr_t, jthr_t = _select_threshold(scoret_ref, coarse_ref, n_kc, n_valid, k_sel, 0)
    to_column = lambda v: jnp.broadcast_to(v, (LANES, tq)).T[:, :1]
    thr, jthr = to_column(thr_t), to_column(jthr_t)

    qs_ref[...] = qa_ref[...] * SM_SCALE
    mx_ref[...] = jnp.full(mx_ref.shape, MASKED, F32)
    acc_ref[...] = jnp.zeros(acc_ref.shape, F32)

    def logits_body(c, carry):
        k0 = pl.multiple_of(c * kc, kc)
        x = score_ref[c]
        sel = jnp.logical_or(x > thr, jnp.logical_and(x == thr, (col0 + k0).astype(F32) <= jthr))
        bias = jnp.where(sel, 0.0, MASKED)
        for g in range(N_KV):
            kt = kvt_ref[0, 2 * g, :, pl.ds(k0, kc)]
            for r in range(GROUP):
                h = g * GROUP + r
                s = jnp.dot(qs_ref[:, h * HEAD_DIM:(h + 1) * HEAD_DIM], kt, preferred_element_type=F32) + bias
                _logits_step(c, h, s, s_ref, mx_ref)
        return carry

    lax.fori_loop(0, n_kc, logits_body, 0)
    _row_max(mx_ref, m_ref)

    def values_body(c, carry):
        k0 = pl.multiple_of(c * kc, kc)
        for g in range(N_KV):
            vt = _with_ones(kvt_ref[0, 2 * g + 1, :, pl.ds(k0, kc)])
            for r in range(GROUP):
                _values_step(c, g * GROUP + r, vt, s_ref, m_ref, acc_ref)
        return carry

    lax.fori_loop(0, n_kc, values_body, 0)
    _softmax_finish(o_ref, acc_ref)


def _dsa_prompt(pp, b, t, tq, kc):
    n_t = t // tq
    k_sel = min(IDX_TOPK, t // 4)
    kern = functools.partial(_dsa_prompt_kernel, k_sel=k_sel)
    return pl.pallas_call(
        kern, grid=(b, n_t),
        in_specs=[pl.BlockSpec((tq, WIDTH), lambda bb, i: (bb * n_t + i, 0)),
                  pl.BlockSpec((tq, WIDTH), lambda bb, i: (bb * n_t + i, 1)),
                  pl.BlockSpec((tq, LANES), lambda bb, i: (bb * n_t + i, 0)),
                  pl.BlockSpec((1, IDX_DIM, t), lambda bb, i: (bb, 0, 0)),
                  pl.BlockSpec((1, KV_SLABS, HEAD_DIM, t), lambda bb, i: (bb, 0, 0, 0))],
        out_specs=pl.BlockSpec((tq, WIDTH), lambda bb, i: (bb * n_t + i, 0)),
        out_shape=jax.ShapeDtypeStruct((b * t, WIDTH), BF16),
        scratch_shapes=[pltpu.VMEM((t // kc, tq, kc), F32), pltpu.VMEM((t // kc, kc, tq), F32),
                        pltpu.VMEM((t // kc, kc, tq), BF16),
                        pltpu.VMEM((tq, WIDTH), BF16),
                        pltpu.VMEM((t // kc, N_HEADS, tq, kc), F32), pltpu.VMEM((N_HEADS, tq, LANES), F32),
                        pltpu.VMEM((N_HEADS, tq, 1), F32), pltpu.VMEM((N_HEADS, tq, ACC_W), F32)],
        compiler_params=_cparams("arbitrary", "arbitrary"), name="dsa_prompt",
    )(pp["q16"], pp["q16"], pp["wi"], pp["kit16"], pp["kvat16"])


def _top_blocks(gate, valid, axis):
    idx = lax.broadcasted_iota(jnp.int32, gate.shape, axis).astype(F32)
    g = jnp.where(valid, gate, -jnp.inf)
    sel = jnp.zeros(gate.shape, F32)
    for _ in range(MOBA_TOPK):
        m = jnp.max(g, axis=axis, keepdims=True)
        first = jnp.min(jnp.where(g == m, idx, float(gate.shape[axis])), axis=axis, keepdims=True)
        pick = idx == first
        sel = jnp.where(jnp.logical_and(pick, m > -jnp.inf), 1.0, sel)
        g = jnp.where(pick, -jnp.inf, g)
    return sel


def _block_indicator(n_rows, n_cols, scale):
    blk = lax.broadcasted_iota(jnp.int32, (n_rows, n_cols), 1) // MOBA_BLOCK
    return jnp.where(blk == lax.broadcasted_iota(jnp.int32, (n_rows, n_cols), 0), scale, 0.0).astype(BF16)


def _moba_prompt_kernel(qb_ref, kvt_ref, o_ref, kext_ref, kmean_ref, qe_ref, s_ref, mx_ref, m_ref, acc_ref):
    own = pl.program_id(1)
    t = kvt_ref.shape[3]
    tq = qb_ref.shape[0]
    n_ext = kext_ref.shape[1] - HEAD_DIM

    @pl.when(own == 0)
    def _():
        ind = _block_indicator(n_ext, t, 1.0)
        for g in range(N_KV):
            kt = kvt_ref[0, 2 * g]
            kext_ref[g, :HEAD_DIM, :] = kt
            kext_ref[g, HEAD_DIM:, :] = ind
            kmean_ref[g] = lax.dot_general(ind, kt, _NT, preferred_element_type=F32) * (1.0 / MOBA_BLOCK)

    blk = lax.broadcasted_iota(jnp.int32, (n_ext, tq), 0)
    for h in range(N_HEADS):
        g = h // GROUP
        q = qb_ref[:, h * HEAD_DIM:(h + 1) * HEAD_DIM]
        gate_t = lax.dot_general(kmean_ref[g].astype(BF16), q, _NT, preferred_element_type=F32)
        picked_t = _top_blocks(gate_t, blk < own, 0)
        allowed = jnp.logical_or(picked_t > 0.5, blk == own)
        bias = _pad_rows(jnp.where(allowed, 0.0, MASKED), LANES).T[:, :n_ext]
        qe_ref[h] = jnp.concatenate([q * SM_SCALE, bias.astype(BF16)], axis=1)

    mx_ref[...] = jnp.full(mx_ref.shape, MASKED, F32)
    acc_ref[...] = jnp.zeros(acc_ref.shape, F32)

    def logits(n, extra_bias):
        k0 = pl.multiple_of(n * MOBA_BLOCK, MOBA_BLOCK)
        for g in range(N_KV):
            ke = kext_ref[g, :, pl.ds(k0, MOBA_BLOCK)]
            for r in range(GROUP):
                h = g * GROUP + r
                s = jnp.dot(qe_ref[h], ke, preferred_element_type=F32)
                _logits_step(n, h, s if extra_bias is None else s + extra_bias, s_ref, mx_ref)

    def logits_body(n, carry):
        logits(n, None)
        return carry

    lax.fori_loop(0, own, logits_body, 0)
    causal = (lax.broadcasted_iota(jnp.int32, (tq, MOBA_BLOCK), 1)
              <= lax.broadcasted_iota(jnp.int32, (tq, MOBA_BLOCK), 0))
    logits(own, jnp.where(causal, 0.0, MASKED))
    _row_max(mx_ref, m_ref)

    def values_body(n, carry):
        k0 = pl.multiple_of(n * MOBA_BLOCK, MOBA_BLOCK)
        for g in range(N_KV):
            vt = _with_ones(kvt_ref[0, 2 * g + 1, :, pl.ds(k0, MOBA_BLOCK)])
            for r in range(GROUP):
                _values_step(n, g * GROUP + r, vt, s_ref, m_ref, acc_ref)
        return carry

    lax.fori_loop(0, own + 1, values_body, 0)
    _softmax_finish(o_ref, acc_ref)


def _moba_prompt(pp, b, t):
    tq = MOBA_BLOCK
    n_t = t // tq
    assert t % MOBA_BLOCK == 0 and n_t <= LANES
    n_ext = -(-n_t // BF16_ROWS) * BF16_ROWS
    return pl.pallas_call(
        _moba_prompt_kernel, grid=(b, n_t),
        in_specs=[pl.BlockSpec((tq, WIDTH), lambda bb, i: (bb * n_t + i, 2)),
                  pl.BlockSpec((1, KV_SLABS, HEAD_DIM, t), lambda bb, i: (bb, 0, 0, 0))],
        out_specs=pl.BlockSpec((tq, WIDTH), lambda bb, i: (bb * n_t + i, 0)),
        out_shape=jax.ShapeDtypeStruct((b * t, WIDTH), BF16),
        scratch_shapes=[pltpu.VMEM((N_KV, HEAD_DIM + n_ext, t), BF16), pltpu.VMEM((N_KV, n_ext, HEAD_DIM), F32),
                        pltpu.VMEM((N_HEADS, tq, HEAD_DIM + n_ext), BF16),
                        pltpu.VMEM((n_t, N_HEADS, tq, MOBA_BLOCK), F32), pltpu.VMEM((N_HEADS, tq, LANES), F32),
                        pltpu.VMEM((N_HEADS, tq, 1), F32), pltpu.VMEM((N_HEADS, tq, ACC_W), F32)],
        compiler_params=_cparams("arbitrary", "arbitrary"), name="moba_prompt",
    )(pp["q16"], pp["kvbt16"])


SEQ_PER_STEP = 2
DECODE_CHUNK = 2048
SELECT_SEQ_PER_STEP = 8


def _page_copy(pt_ref, pool, buf, sems, j, seq, slot, u, p):
    off = pl.multiple_of(p * PAGE_SIZE, PAGE_SIZE)
    lead = (slice(None),) * (len(buf.shape) - 3)
    return pltpu.make_async_copy(pool.at[pt_ref[seq, p]], buf.at[(slot, u) + lead + (pl.ds(off, PAGE_SIZE),)],
                                 sems.at[2 * j + slot])


def _paged_fetch(pt_ref, pools, bufs, sems, past):
    b = pl.program_id(0)
    nb = pl.num_programs(0)
    slot = b % 2
    n_pages = past // PAGE_SIZE
    n_seq = bufs[0].shape[1]

    def each_copy(step, sl, p, fn):
        for u in range(n_seq):
            for j, (pool, buf) in enumerate(zip(pools, bufs)):
                fn(_page_copy(pt_ref, pool, buf, sems, j, step * n_seq + u, sl, u, p))

    def start(step, sl):
        def body(p, carry):
            each_copy(step, sl, p, lambda cp: cp.start())
            return carry
        lax.fori_loop(0, n_pages, body, 0)

    @pl.when(b == 0)
    def _():
        start(0, 0)

    @pl.when(b + 1 < nb)
    def _():
        start(b + 1, 1 - slot)

    def wait_body(p, carry):
        each_copy(b, slot, p, lambda cp: cp.wait())
        return carry
    lax.fori_loop(0, n_pages, wait_body, 0)
    return slot


def _stack_heads(q, heads, scale=1.0):
    qf = q.astype(F32) * scale
    return jnp.concatenate([qf[:, h * HEAD_DIM:(h + 1) * HEAD_DIM] for h in heads], axis=0).astype(BF16)


def _pad_rows(x, n):
    return jnp.concatenate([x, jnp.zeros((n - x.shape[0], x.shape[1]), x.dtype)], axis=0)


def _chunked_attention(queries, kt_of, vt_of, bias_of, bias_new, k_new, v_new, s_ref, chunk):
    n = len(queries)
    rows = queries[0].shape[0]
    n_chunks = s_ref.shape[2] // chunk

    def logits_body(c, mx):
        k0 = pl.multiple_of(c * chunk, chunk)
        out = []
        for i in range(n):
            s = jnp.dot(queries[i], kt_of(i, k0), preferred_element_type=F32) + bias_of(i, k0)
            s_ref[i, :, pl.ds(k0, chunk)] = s
            out.append(jnp.maximum(mx[i], _fold_lanes(s, jnp.maximum)))
        return tuple(out)

    mx = lax.fori_loop(0, n_chunks, logits_body, tuple(jnp.full((rows, LANES), MASKED, F32) for _ in range(n)))
    s_new = [lax.dot_general(queries[i], k_new[i], _NT, preferred_element_type=F32) + bias_new[i] for i in range(n)]
    m = [jnp.maximum(jnp.max(mx[i], axis=1, keepdims=True), jnp.max(s_new[i], axis=1, keepdims=True))
         for i in range(n)]

    def values_body(c, carry):
        k0 = pl.multiple_of(c * chunk, chunk)
        out = []
        for i in range(n):
            l, acc = carry[i]
            p = jnp.exp(s_ref[i, :, pl.ds(k0, chunk)] - m[i])
            out.append((l + _fold_lanes(p, jnp.add),
                        acc + lax.dot_general(p.astype(BF16), vt_of(i, k0), _NT, preferred_element_type=F32)))
        return tuple(out)

    init = tuple((jnp.zeros((rows, LANES), F32), jnp.zeros((rows, HEAD_DIM), F32)) for _ in range(n))
    carry = lax.fori_loop(0, n_chunks, values_body, init)
    outs = []
    for i in range(n):
        l, acc = carry[i]
        p_new = jnp.exp(s_new[i] - m[i])
        l = jnp.sum(l, axis=1, keepdims=True) + jnp.sum(p_new, axis=1, keepdims=True)
        acc = acc + jnp.dot(p_new.astype(BF16), v_new[i], preferred_element_type=F32)
        outs.append(acc / l)
    return outs


def _store_heads(o_ref, u, per_group, nq):
    outs = [per_group[g][r * nq:(r + 1) * nq, :] for g in range(N_KV) for r in range(GROUP)]
    o_ref[u] = jnp.concatenate(outs, axis=1).astype(o_ref.dtype)


def _new_kv(kv_new, g):
    k = kv_new[:, 2 * g * HEAD_DIM:(2 * g + 1) * HEAD_DIM]
    v = kv_new[:, (2 * g + 1) * HEAD_DIM:(2 * g + 2) * HEAD_DIM]
    return _pad_rows(k, LANES).astype(BF16), _pad_rows(v, LANES).astype(BF16)


def _units():
    return [(u, g) for u in range(SEQ_PER_STEP) for g in range(N_KV)]


def _dsa_select_kernel(pt_ref, q_ref, kiw_ref, idx_pool, bias_ref, idxbuf, score_ref, coarse_ref, sems, *,
                       past, k_sel, chunk):
    n_seq, nq = q_ref.shape[0], q_ref.shape[1]
    slot = _paged_fetch(pt_ref, (idx_pool,), (idxbuf,), sems, past)
    col_new = lax.broadcasted_iota(jnp.int32, (1, LANES), 1)
    qidx = lax.broadcasted_iota(jnp.int32, (nq, 1), 0)
    qi_rows = [_stack_heads(q_ref[u], range(IDX_HEADS)) for u in range(n_seq)]
    kiw = [kiw_ref[u] for u in range(n_seq)]

    def head_sum(s, u):
        score = jnp.zeros((nq, s.shape[1]), F32)
        for h in range(IDX_HEADS):
            score = score + jnp.maximum(s[h * nq:(h + 1) * nq, :], 0.0) * kiw[u][:, IDX_DIM + h:IDX_DIM + h + 1]
        return score

    def idx_body(c, carry):
        k0 = pl.multiple_of(c * chunk, chunk)
        for u in range(n_seq):
            keys = idxbuf[slot, u, :, pl.ds(k0, chunk)].astype(BF16)
            score_ref[0, u * nq:(u + 1) * nq, pl.ds(k0, chunk)] = head_sum(
                jnp.dot(qi_rows[u], keys, preferred_element_type=F32), u)
        return carry

    lax.fori_loop(0, past // chunk, idx_body, 0)
    for u in range(n_seq):
        k_new = _pad_rows(kiw[u][:, :IDX_DIM], LANES).astype(BF16)
        s = head_sum(lax.dot_general(qi_rows[u], k_new, _NT, preferred_element_type=F32), u)
        score_ref[0, u * nq:(u + 1) * nq, past:] = jnp.where(col_new <= qidx, s, -jnp.inf)

    n_valid = jnp.concatenate([past + qidx + 1] * n_seq, axis=0)
    coarse_ref[0] = score_ref[0].astype(BF16)
    thr, jthr = _select_threshold(score_ref, coarse_ref, 1, n_valid, k_sel, 1)
    x = score_ref[0]
    col = lax.broadcasted_iota(jnp.int32, (1, x.shape[1]), 1).astype(F32)
    sel = jnp.logical_or(x > thr, jnp.logical_and(x == thr, col <= jthr))
    bias = jnp.where(sel, 0.0, MASKED)
    for u in range(n_seq):
        bias_ref[u] = bias[u * nq:(u + 1) * nq, :]


def _dsa_decode_kernel(pt_ref, q_ref, kva_ref, bias_ref, kv_pool, o_ref, kvbuf, s_ref, sems, *, past, chunk):
    nq = q_ref.shape[1]
    slot = _paged_fetch(pt_ref, (kv_pool,), (kvbuf,), sems, past)

    def bias_rows(u, k0, width):
        return jnp.concatenate([bias_ref[u, :, pl.ds(k0, width)]] * GROUP, axis=0)

    units = _units()
    queries = [_stack_heads(q_ref[u], range(g * GROUP, (g + 1) * GROUP), SM_SCALE) for u, g in units]
    new = [_new_kv(kva_ref[u], g) for u, g in units]
    outs = _chunked_attention(
        queries,
        lambda i, k0: kvbuf[slot, units[i][0], 2 * units[i][1], :, pl.ds(k0, chunk)].astype(BF16),
        lambda i, k0: kvbuf[slot, units[i][0], 2 * units[i][1] + 1, :, pl.ds(k0, chunk)].astype(BF16),
        lambda i, k0: bias_rows(units[i][0], k0, chunk),
        [bias_rows(u, past, LANES) for u, g in units], [kv[0] for kv in new], [kv[1] for kv in new],
        s_ref, chunk)
    for u in range(SEQ_PER_STEP):
        _store_heads(o_ref, u, outs[u * N_KV:(u + 1) * N_KV], nq)


def _moba_decode_kernel(pt_ref, q_ref, kvb_ref, kv_pool, o_ref, kvbuf, s_ref, sems, *, past, chunk):
    nq = q_ref.shape[1]
    n_blk = past // MOBA_BLOCK
    n_chunks = past // chunk
    slot = _paged_fetch(pt_ref, (kv_pool,), (kvbuf,), sems, past)
    rows = GROUP * nq
    units = _units()
    kt_of = lambda i, k0: kvbuf[slot, units[i][0], 2 * units[i][1], :, pl.ds(k0, chunk)].astype(BF16)

    def indicator(k0):
        blk = (k0 + lax.broadcasted_iota(jnp.int32, (n_blk, chunk), 1)) // MOBA_BLOCK
        return jnp.where(blk == lax.broadcasted_iota(jnp.int32, (n_blk, chunk), 0), 1.0, 0.0).astype(BF16)

    def mean_body(c, sums):
        k0 = pl.multiple_of(c * chunk, chunk)
        ind = indicator(k0)
        return tuple(sums[i] + lax.dot_general(ind, kt_of(i, k0), _NT, preferred_element_type=F32)
                     for i in range(len(units)))

    sums = lax.fori_loop(0, n_chunks, mean_body, tuple(jnp.zeros((n_blk, HEAD_DIM), F32) for _ in units))

    queries, bias_blk = [], []
    for i, (u, g) in enumerate(units):
        qb = q_ref[u]
        kmean = (sums[i] * (1.0 / MOBA_BLOCK)).astype(BF16)
        gate = lax.dot_general(_stack_heads(qb, range(g * GROUP, (g + 1) * GROUP)), kmean, _NT,
                               preferred_element_type=F32)
        picked = _top_blocks(gate, jnp.full(gate.shape, True), 1)
        bias_blk.append(jnp.where(picked > 0.5, 0.0, MASKED).astype(BF16))
        queries.append(_stack_heads(qb, range(g * GROUP, (g + 1) * GROUP), SM_SCALE))

    col = lax.broadcasted_iota(jnp.int32, (rows, LANES), 1)
    qidx = lax.broadcasted_iota(jnp.int32, (rows, LANES), 0) % nq
    bias_new = jnp.where(col <= qidx, 0.0, MASKED)
    new = [_new_kv(kvb_ref[u], g) for u, g in units]
    outs = _chunked_attention(
        queries, kt_of,
        lambda i, k0: kvbuf[slot, units[i][0], 2 * units[i][1] + 1, :, pl.ds(k0, chunk)].astype(BF16),
        lambda i, k0: jnp.dot(bias_blk[i], indicator(k0), preferred_element_type=F32),
        [bias_new] * len(units), [kv[0] for kv in new], [kv[1] for kv in new], s_ref, chunk)
    for u in range(SEQ_PER_STEP):
        _store_heads(o_ref, u, outs[u * N_KV:(u + 1) * N_KV], nq)


def _sample_specs(n_seq, nq, widths_blocks):
    return [pl.BlockSpec((n_seq, nq, w), lambda bb, pt, j=j: (bb, 0, j)) for w, j in widths_blocks]


def _dsa_decode(ps, page_table, idx_pool_t, kv_pool_t, db, nq):
    past = page_table.shape[1] * PAGE_SIZE
    lp = past + LANES
    k_sel = min(IDX_TOPK, (past + nq) // 4)
    chunk = math.gcd(past, DECODE_CHUNK)
    q16 = ps["q16"].reshape(db, nq, -1)
    n_sel = math.gcd(db, SELECT_SEQ_PER_STEP)
    assert db % SEQ_PER_STEP == 0

    bias = pl.pallas_call(
        functools.partial(_dsa_select_kernel, past=past, k_sel=k_sel, chunk=chunk),
        grid_spec=pltpu.PrefetchScalarGridSpec(
            num_scalar_prefetch=1, grid=(db // n_sel,),
            in_specs=_sample_specs(n_sel, nq, [(WIDTH, 1), (LANES, 0)]) + [pl.BlockSpec(memory_space=pl.ANY)],
            out_specs=_sample_specs(n_sel, nq, [(lp, 0)])[0],
            scratch_shapes=[pltpu.VMEM((2, n_sel, IDX_DIM, past), F32), pltpu.VMEM((1, n_sel * nq, lp), F32),
                            pltpu.VMEM((1, n_sel * nq, lp), BF16),
                            pltpu.SemaphoreType.DMA((2,))]),
        out_shape=jax.ShapeDtypeStruct((db, nq, lp), F32),
        compiler_params=_cparams("arbitrary"), name="dsa_select",
    )(page_table, q16, ps["kiw"].reshape(db, nq, LANES), idx_pool_t)

    return pl.pallas_call(
        functools.partial(_dsa_decode_kernel, past=past, chunk=chunk),
        grid_spec=pltpu.PrefetchScalarGridSpec(
            num_scalar_prefetch=1, grid=(db // SEQ_PER_STEP,),
            in_specs=_sample_specs(SEQ_PER_STEP, nq, [(WIDTH, 0), (KV_WIDTH, 0), (lp, 0)])
            + [pl.BlockSpec(memory_space=pl.ANY)],
            out_specs=_sample_specs(SEQ_PER_STEP, nq, [(WIDTH, 0)])[0],
            scratch_shapes=[pltpu.VMEM((2, SEQ_PER_STEP, KV_SLABS, HEAD_DIM, past), F32),
                            pltpu.VMEM((SEQ_PER_STEP * N_KV, GROUP * nq, past), F32),
                            pltpu.SemaphoreType.DMA((2,))]),
        out_shape=jax.ShapeDtypeStruct((db, nq, WIDTH), BF16),
        compiler_params=_cparams("arbitrary"), name="dsa_decode",
    )(page_table, q16, ps["kva"].reshape(db, nq, KV_WIDTH), bias, kv_pool_t)


def _moba_decode(ps, page_table, kv_pool_t, db, nq):
    past = page_table.shape[1] * PAGE_SIZE
    assert past % MOBA_BLOCK == 0 and nq <= MOBA_BLOCK and db % SEQ_PER_STEP == 0
    qw = ps["q16"].shape[1]
    kern = functools.partial(_moba_decode_kernel, past=past, chunk=math.gcd(past, DECODE_CHUNK))
    grid_spec = pltpu.PrefetchScalarGridSpec(
        num_scalar_prefetch=1, grid=(db // SEQ_PER_STEP,),
        in_specs=_sample_specs(SEQ_PER_STEP, nq, [(WIDTH, 2), (KV_WIDTH, 0)]) + [pl.BlockSpec(memory_space=pl.ANY)],
        out_specs=_sample_specs(SEQ_PER_STEP, nq, [(WIDTH, 0)])[0],
        scratch_shapes=[pltpu.VMEM((2, SEQ_PER_STEP, KV_SLABS, HEAD_DIM, past), F32),
                        pltpu.VMEM((SEQ_PER_STEP * N_KV, GROUP * nq, past), F32), pltpu.SemaphoreType.DMA((2,))])
    return pl.pallas_call(
        kern, grid_spec=grid_spec, out_shape=jax.ShapeDtypeStruct((db, nq, WIDTH), BF16),
        compiler_params=_cparams("arbitrary"), name="moba_decode",
    )(page_table, ps["q16"].reshape(db, nq, qw), ps["kvb"].reshape(db, nq, KV_WIDTH), kv_pool_t)


def _out_kernel(x_ref, oa_ref, ob_ref, gate_ref, wba_ref, wbb_ref, wo_ref, y_ref):
    d = x_ref.shape[1]
    ua = oa_ref[...] * gate_ref[:, 0:WIDTH]
    ub = ob_ref[...] * gate_ref[:, WIDTH:2 * WIDTH]
    a = jnp.dot(ua, wba_ref[...], preferred_element_type=F32)
    b = jnp.dot(ub, wbb_ref[...], preferred_element_type=F32)
    ga = gate_ref[:, 2 * WIDTH:2 * WIDTH + d].astype(F32)
    gb = gate_ref[:, 2 * WIDTH + d:2 * WIDTH + 2 * d].astype(F32)
    merged = (ga * a + gb * b).astype(BF16)
    y_ref[...] = x_ref[...] + jnp.dot(merged, wo_ref[...], preferred_element_type=F32)


def _out_proj(x2d, oa, ob, gate16, wts, tm):
    n, d = x2d.shape
    row = lambda w: pl.BlockSpec((tm, w), lambda i: (i, 0))
    return pl.pallas_call(
        _out_kernel, grid=(n // tm,),
        in_specs=[row(d), row(WIDTH), row(WIDTH), row(gate16.shape[1]),
                  _const_spec((WIDTH, d)), _const_spec((WIDTH, d)), _const_spec((d, d))],
        out_specs=row(d), out_shape=jax.ShapeDtypeStruct((n, d), F32),
        compiler_params=_cparams("arbitrary"), name="out_proj",
    )(x2d, oa, ob, gate16, wts["wba"], wts["wbb"], wts["wo"])


def _rope_angles(pos):
    inv = ROPE_THETA ** (-jnp.arange(ROT_HALF, dtype=F32) / ROT_HALF)
    ang = pos.astype(F32)[:, None] * inv[None, :]
    return jnp.cos(ang), jnp.sin(ang)


def _rope_tables(pos):
    p = pos.shape[0]
    c, s = _rope_angles(pos)
    rest = HEAD_DIM - 2 * ROT_HALF
    z8, zr = jnp.zeros((p, ROT_HALF), F32), jnp.zeros((p, rest), F32)
    c_head = jnp.concatenate([c, c, jnp.ones((p, rest), F32)], axis=1)
    s1_head = jnp.concatenate([-s, z8, zr], axis=1)
    s2_head = jnp.concatenate([z8, s, zr], axis=1)
    one, zero = jnp.ones((p, HEAD_DIM), F32), jnp.zeros((p, HEAD_DIM), F32)
    cat = lambda a, b: jnp.concatenate([a, b], axis=1)
    return (cat(c_head, c_head), cat(s1_head, s1_head), cat(s2_head, s2_head),
            cat(c_head, one), cat(s1_head, zero), cat(s2_head, zero))


def _prep_weights(norm_g, w_in, qn_a, kn_a, qn_i, kn_i, qn_b, kn_b, w_ba, w_bb, w_out):
    d = w_in.shape[0]
    kvw = N_KV * HEAD_DIM
    splits = (WIDTH, kvw, kvw, IDX_HEADS * IDX_DIM, IDX_DIM, IDX_HEADS, WIDTH, WIDTH, kvw, kvw, WIDTH, d, d)
    offs = np.concatenate([[0], np.cumsum(splits)])
    qa, ka, va, qi, ki, wi, za, qb, kb, vb, zb, ga, gb = [w_in[:, offs[j]:offs[j + 1]] for j in range(13)]
    hd = HEAD_DIM
    pad = jnp.zeros((d, LANES - IDX_DIM - IDX_HEADS), F32)
    slabs = jnp.concatenate([ka[:, :hd], va[:, :hd], ka[:, hd:], va[:, hd:],
                             kb[:, :hd], vb[:, :hd], kb[:, hd:], vb[:, hd:], ki], axis=1)
    wk = jnp.concatenate([slabs, wi, pad], axis=1)
    wwi = jnp.concatenate([wi, jnp.zeros((d, LANES - IDX_HEADS), F32)], axis=1)
    zero = jnp.zeros((hd,), F32)
    one = jnp.ones((hd,), F32)
    wi_gain = jnp.concatenate([jnp.full((IDX_HEADS,), (IDX_HEADS * IDX_DIM) ** -0.5, F32),
                               jnp.zeros((hd - IDX_HEADS,), F32)])
    gka = jnp.concatenate([kn_a, zero, kn_a, zero, kn_b, zero, kn_b, zero, kn_i, zero])[None, :]
    gkb = jnp.concatenate([zero, one, zero, one, zero, one, zero, one, zero, wi_gain])[None, :]
    gkt = jnp.concatenate([kn_a, one, kn_a, one, kn_b, one, kn_b, one, kn_i])[:, None]
    gq = jnp.concatenate([jnp.tile(qn_a, N_HEADS), jnp.tile(qn_i, IDX_HEADS), jnp.tile(qn_b, N_HEADS)])[None, :]
    blk = np.arange(MXU_DIM) // HEAD_DIM
    bd = jnp.asarray((blk[:, None] == blk[None, :]) / HEAD_DIM, BF16)
    return dict(
        ng=norm_g[None, :].astype(F32),
        wq=jnp.concatenate([qa, qi, qb], axis=1).astype(BF16), wk=wk.astype(BF16),
        wkt=slabs.T.astype(BF16), wwi=wwi.astype(BF16),
        wg=jnp.concatenate([za, zb, ga, gb], axis=1).astype(BF16),
        gq=gq, gka=gka, gkb=gkb, gkt=gkt, bd=bd,
        wba=w_ba.astype(BF16), wbb=w_bb.astype(BF16), wo=w_out.astype(BF16))


def kernel(x_prompt, x_sample, cache_kv_a, cache_idx_k, cache_kv_b, page_table, norm_g, w_in, q_norm_a, k_norm_a,
           idx_q_norm, idx_k_norm, q_norm_b, k_norm_b, w_branch_a, w_branch_b, w_out):
    b, t, d = x_prompt.shape
    db, nq, _ = x_sample.shape
    depth = w_in.shape[0]
    n_phys = cache_kv_a.shape[1]
    past = page_table.shape[1] * PAGE_SIZE
    tm = 256
    tm_s = min(tm, db * nq)
    assert t % tm == 0 and tm_s % nq == 0 and (db * nq) % tm_s == 0

    pos_p = jnp.arange(t, dtype=jnp.int32)
    cos_p, sin_p = _rope_angles(pos_p)
    tab_p = _rope_tables(pos_p)[:3] + (cos_p.T, sin_p.T)
    tab_s = tuple(jnp.tile(a, (tm_s // nq, 1)) for a in _rope_tables(past + jnp.arange(nq, dtype=jnp.int32)))

    kv_t = lambda pool: jnp.transpose(pool, (0, 2, 3, 4, 1)).reshape(n_phys, KV_SLABS, HEAD_DIM, PAGE_SIZE)

    hp = x_prompt.reshape(b * t, d)
    hs = x_sample.reshape(db * nq, d)
    new = [[] for _ in range(6)]
    for l in range(depth):
        wts = _prep_weights(norm_g[l], w_in[l], q_norm_a[l], k_norm_a[l], idx_q_norm[l], idx_k_norm[l],
                            q_norm_b[l], k_norm_b[l], w_branch_a[l], w_branch_b[l], w_out[l])
        pp = _project_prompt(hp, tab_p, wts, b, t, tm)
        oa = _dsa_prompt(pp, b, t, tq=256, kc=min(512, t))
        ob = _moba_prompt(pp, b, t)
        hp = _out_proj(hp, oa, ob, pp["gate16"], wts, tm)

        ps = _project_sample(hs, tab_s, wts, tm_s)
        oa_s = _dsa_decode(ps, page_table, jnp.transpose(cache_idx_k[l], (0, 2, 1)), kv_t(cache_kv_a[l]), db, nq)
        ob_s = _moba_decode(ps, page_table, kv_t(cache_kv_b[l]), db, nq)
        hs = _out_proj(hs, oa_s.reshape(db * nq, WIDTH), ob_s.reshape(db * nq, WIDTH), ps["gate16"], wts, tm_s)

        to_tokens = lambda a: jnp.transpose(a.reshape(b, N_KV, 2, HEAD_DIM, t), (0, 4, 1, 2, 3))
        new[0].append(to_tokens(pp["kvat"]))
        new[1].append(jnp.transpose(pp["kit"], (0, 2, 1)))
        new[2].append(to_tokens(pp["kvbt"]))
        new[3].append(ps["kva"].reshape(db, nq, N_KV, 2, HEAD_DIM))
        new[4].append(ps["kiw"][:, :IDX_DIM].reshape(db, nq, IDX_DIM))
        new[5].append(ps["kvb"].reshape(db, nq, N_KV, 2, HEAD_DIM))
    return (hp.reshape(b, t, d), hs.reshape(db, nq, d)) + tuple(jnp.stack(a, axis=0) for a in new)
```

```python
import functools
import math

import numpy as np
import jax
import jax.numpy as jnp
from jax import lax
from jax.experimental import pallas as pl
from jax.experimental.pallas import tpu as pltpu

F32 = jnp.float32
BF16 = jnp.bfloat16

HEAD_DIM = 64
N_HEADS = 8
N_KV = 2
GROUP = N_HEADS // N_KV
WIDTH = N_HEADS * HEAD_DIM
KV_WIDTH = N_KV * 2 * HEAD_DIM
KV_SLABS = N_KV * 2
IDX_HEADS = 8
IDX_DIM = 64
IDX_TOPK = 256
MOBA_BLOCK = 256
MOBA_TOPK = 3
PAGE_SIZE = 128
ROPE_THETA = 500000.0
ROT_HALF = HEAD_DIM // 4 // 2
RMS_EPS = 1e-6
SM_SCALE = HEAD_DIM ** -0.5

LANES = 128
SUBLANES = 8
MXU_DIM = 256
BF16_ROWS = 16
VMEM_LIMIT_BYTES = 56 * 1024 * 1024

MASKED = -1e30
BISECT_STEPS = 6

_NT = (((1,), (1,)), ((), ()))


def _cparams(*sem):
    return pltpu.CompilerParams(dimension_semantics=sem, vmem_limit_bytes=VMEM_LIMIT_BYTES)


def _normed_input(x_ref, ng_ref):
    x = x_ref[...]
    inv = lax.rsqrt(jnp.mean(x * x, axis=-1, keepdims=True) + RMS_EPS)
    return (x * inv * ng_ref[...]).astype(BF16)


def _head_mean_sq(z, bd_ref):
    parts = []
    for c in range(0, z.shape[1], MXU_DIM):
        cw = min(MXU_DIM, z.shape[1] - c)
        zc = z[:, c:c + cw]
        parts.append(jnp.dot((zc * zc).astype(BF16), bd_ref[:cw, :cw], preferred_element_type=F32))
    return parts[0] if len(parts) == 1 else jnp.concatenate(parts, axis=1)


def _rope_lanes(y, c_ref, s1_ref, s2_ref):
    return (y * c_ref[...] + pltpu.roll(y, LANES - ROT_HALF, 1) * s1_ref[...]
            + pltpu.roll(y, ROT_HALF, 1) * s2_ref[...])


def _queries_and_gates(h, wq_ref, wg_ref, gq_ref, bd_ref, cq_ref, s1q_ref, s2q_ref, q16_ref, gate16_ref):
    for c0 in range(0, q16_ref.shape[1], 512):
        z = jnp.dot(h, wq_ref[:, c0:c0 + 512], preferred_element_type=F32)
        y = z * lax.rsqrt(_head_mean_sq(z, bd_ref) + RMS_EPS) * gq_ref[:, c0:c0 + 512]
        for j in range(0, 512, LANES):
            q16_ref[:, c0 + j:c0 + j + LANES] = _rope_lanes(y[:, j:j + LANES], cq_ref, s1q_ref, s2q_ref).astype(BF16)
    n_silu = 2 * WIDTH
    for c0 in range(0, gate16_ref.shape[1], 512):
        z = jnp.dot(h, wg_ref[:, c0:c0 + 512], preferred_element_type=F32)
        sig = 1.0 / (1.0 + jnp.exp(-z))
        gate16_ref[:, c0:c0 + 512] = (z * sig if c0 < n_silu else sig).astype(BF16)


def _proj_sample_kernel(x_ref, ng_ref, wq_ref, wk_ref, wg_ref, gq_ref, gka_ref, gkb_ref, bd_ref,
                        cq_ref, s1q_ref, s2q_ref, ck_ref, s1k_ref, s2k_ref,
                        q16_ref, gate16_ref, kva_ref, kvb_ref, kiw_ref):
    h = _normed_input(x_ref, ng_ref)
    _queries_and_gates(h, wq_ref, wg_ref, gq_ref, bd_ref, cq_ref, s1q_ref, s2q_ref, q16_ref, gate16_ref)
    z = jnp.dot(h, wk_ref[...], preferred_element_type=F32)
    y = z * (lax.rsqrt(_head_mean_sq(z, bd_ref) + RMS_EPS) * gka_ref[...] + gkb_ref[...])
    r = [_rope_lanes(y[:, j * LANES:(j + 1) * LANES], ck_ref, s1k_ref, s2k_ref) for j in range(5)]
    for j in range(2):
        kva_ref[:, j * LANES:(j + 1) * LANES] = r[j]
        kvb_ref[:, j * LANES:(j + 1) * LANES] = r[2 + j]
    kiw_ref[...] = r[4]


def _proj_prompt_kernel(x_ref, ng_ref, wq_ref, wkt_ref, wwi_ref, wg_ref, gq_ref, gkt_ref, bd_ref,
                        cq_ref, s1q_ref, s2q_ref, ct_ref, st_ref,
                        q16_ref, gate16_ref, wi_ref, kvat_ref, kvbt_ref, kit_ref, kvat16_ref, kvbt16_ref, kit16_ref):
    h = _normed_input(x_ref, ng_ref)
    _queries_and_gates(h, wq_ref, wg_ref, gq_ref, bd_ref, cq_ref, s1q_ref, s2q_ref, q16_ref, gate16_ref)
    wi_ref[...] = jnp.dot(h, wwi_ref[...], preferred_element_type=F32) * (IDX_HEADS * IDX_DIM) ** -0.5

    zt = lax.dot_general(wkt_ref[...], h, _NT, preferred_element_type=F32)
    cos, sin = ct_ref[...], st_ref[...]

    def key_head(j):
        z = zt[j * HEAD_DIM:(j + 1) * HEAD_DIM, :]
        y = z * lax.rsqrt(jnp.mean(z * z, axis=0, keepdims=True) + RMS_EPS) * gkt_ref[j * HEAD_DIM:(j + 1) * HEAD_DIM, :]
        y1, y2 = y[:ROT_HALF], y[ROT_HALF:2 * ROT_HALF]
        return jnp.concatenate([y1 * cos - y2 * sin, y2 * cos + y1 * sin, y[2 * ROT_HALF:]], axis=0)

    for j in range(KV_SLABS):
        for base, out_ref, out16_ref in ((0, kvat_ref, kvat16_ref), (KV_SLABS, kvbt_ref, kvbt16_ref)):
            slab = key_head(base + j) if j % 2 == 0 else zt[(base + j) * HEAD_DIM:(base + j + 1) * HEAD_DIM, :]
            out_ref[0, j] = slab
            out16_ref[0, j] = slab.astype(BF16)
    ki = key_head(2 * KV_SLABS)
    kit_ref[0] = ki
    kit16_ref[0] = ki.astype(BF16)


def _const_spec(shape, n_grid=1):
    if n_grid == 1:
        return pl.BlockSpec(shape, lambda i: (0,) * len(shape))
    return pl.BlockSpec(shape, lambda b, i: (0,) * len(shape))


def _project_sample(x2d, tables, wts, tm):
    n, d = x2d.shape
    wq, wk, wg = wts["wq"], wts["wk"], wts["wg"]
    row = lambda w: pl.BlockSpec((tm, w), lambda i: (i, 0))
    tab = pl.BlockSpec((tm, LANES), lambda i: (0, 0))
    in_specs = [row(d), _const_spec((1, d)), _const_spec(wq.shape), _const_spec(wk.shape), _const_spec(wg.shape),
                _const_spec((1, wq.shape[1])), _const_spec((1, wk.shape[1])), _const_spec((1, wk.shape[1])),
                _const_spec((MXU_DIM, MXU_DIM))] + [tab] * 6
    out_shape = [
        jax.ShapeDtypeStruct((n, wq.shape[1]), BF16),
        jax.ShapeDtypeStruct((n, wg.shape[1]), BF16),
        jax.ShapeDtypeStruct((n, KV_WIDTH), F32),
        jax.ShapeDtypeStruct((n, KV_WIDTH), F32),
        jax.ShapeDtypeStruct((n, LANES), F32),
    ]
    outs = pl.pallas_call(
        _proj_sample_kernel, grid=(n // tm,), in_specs=in_specs, out_specs=[row(s.shape[1]) for s in out_shape],
        out_shape=out_shape, compiler_params=_cparams("arbitrary"), name="project_sample",
    )(x2d, wts["ng"], wq, wk, wg, wts["gq"], wts["gka"], wts["gkb"], wts["bd"], *tables)
    return dict(zip(("q16", "gate16", "kva", "kvb", "kiw"), outs))


def _project_prompt(x2d, tables, wts, b, t, tm):
    n, d = x2d.shape
    n_t = t // tm
    wq, wkt, wwi, wg = wts["wq"], wts["wkt"], wts["wwi"], wts["wg"]
    row = lambda w: pl.BlockSpec((tm, w), lambda bb, i: (bb * n_t + i, 0))
    tab = pl.BlockSpec((tm, LANES), lambda bb, i: (i, 0))
    tab_t = pl.BlockSpec((ROT_HALF, tm), lambda bb, i: (0, i))
    cs = lambda shape: _const_spec(shape, 2)
    in_specs = [row(d), cs((1, d)), cs(wq.shape), cs(wkt.shape), cs(wwi.shape), cs(wg.shape),
                cs((1, wq.shape[1])), cs((wkt.shape[0], 1)), cs((MXU_DIM, MXU_DIM)), tab, tab, tab, tab_t, tab_t]
    kv_t = lambda dt: jax.ShapeDtypeStruct((b, KV_SLABS, HEAD_DIM, t), dt)
    ki_t = lambda dt: jax.ShapeDtypeStruct((b, IDX_DIM, t), dt)
    out_shape = [jax.ShapeDtypeStruct((n, wq.shape[1]), BF16), jax.ShapeDtypeStruct((n, wg.shape[1]), BF16),
                 jax.ShapeDtypeStruct((n, LANES), F32),
                 kv_t(F32), kv_t(F32), ki_t(F32), kv_t(BF16), kv_t(BF16), ki_t(BF16)]
    kv_spec = pl.BlockSpec((1, KV_SLABS, HEAD_DIM, tm), lambda bb, i: (bb, 0, 0, i))
    ki_spec = pl.BlockSpec((1, IDX_DIM, tm), lambda bb, i: (bb, 0, i))
    out_specs = [row(wq.shape[1]), row(wg.shape[1]), row(LANES), kv_spec, kv_spec, ki_spec, kv_spec, kv_spec, ki_spec]
    outs = pl.pallas_call(
        _proj_prompt_kernel, grid=(b, n_t), in_specs=in_specs, out_specs=out_specs, out_shape=out_shape,
        compiler_params=_cparams("arbitrary", "arbitrary"), name="project_prompt",
    )(x2d, wts["ng"], wq, wkt, wwi, wg, wts["gq"], wts["gkt"], wts["bd"], *tables)
    return dict(zip(("q16", "gate16", "wi", "kvat", "kvbt", "kit", "kvat16", "kvbt16", "kit16"), outs))


def _fold_lanes(x, op):
    acc = x[:, :LANES]
    for j in range(1, x.shape[1] // LANES):
        acc = op(acc, x[:, j * LANES:(j + 1) * LANES])
    return acc


def _fold_sublanes(x, op):
    n = x.shape[0] // SUBLANES
    parts = [x[j * SUBLANES:(j + 1) * SUBLANES, :] for j in range(min(n, SUBLANES))]
    for j in range(len(parts), n):
        parts[j % SUBLANES] = op(parts[j % SUBLANES], x[j * SUBLANES:(j + 1) * SUBLANES, :])
    while len(parts) > 1:
        parts = [op(parts[j], parts[j + 1]) for j in range(0, len(parts) - 1, 2)] + parts[len(parts) & ~1:]
    return parts[0]


def _select_threshold(score_ref, n_chunks, n_valid, k, key_axis):
    width = score_ref.shape[1 + key_axis]
    rows = score_ref.shape[2 - key_axis]
    kf = float(k)
    neg, pos = -jnp.inf, jnp.inf
    fold = _fold_lanes if key_axis == 1 else _fold_sublanes
    acc_shape = (rows, LANES) if key_axis == 1 else (SUBLANES, rows)

    def reduce_all(fn, op, init, final_reduce):
        def body(c, acc):
            return op(acc, fold(fn(score_ref[c], c), op))
        acc = lax.fori_loop(0, n_chunks, body, jnp.full(acc_shape, init, F32))
        return final_reduce(acc, axis=key_axis, keepdims=True)

    def count(pred):
        return reduce_all(lambda x, c: jnp.where(pred(x, c), 1.0, 0.0), jnp.add, 0.0, jnp.sum)

    def max_below(v):
        return reduce_all(lambda x, c: jnp.where(x < v, x, neg), jnp.maximum, neg, jnp.max)

    small = n_valid <= k
    row_min = reduce_all(lambda x, c: jnp.where(x > neg, x, pos), jnp.minimum, pos, jnp.min)

    def cond(s):
        return jnp.sum(1.0 - s[5]) > 0.0

    def finished(c_lo, c_hi, done):
        return jnp.where(jnp.logical_or(c_hi == kf - 1.0, c_lo == kf), 1.0, done)

    def body(s):
        lo, hi, c_lo, c_hi, snap, done = s
        vmax = max_below(hi)
        c = count(lambda x, cc: x >= vmax)
        live = done < 0.5
        hit = jnp.logical_and(live, c >= kf)
        move = jnp.logical_and(live, c < kf)
        snap = jnp.where(hit, 1.0, snap)
        hi = jnp.where(move, vmax, hi)
        c_hi = jnp.where(move, c, c_hi)
        done = finished(c_lo, c_hi, jnp.where(hit, 1.0, done))
        for _ in range(BISECT_STEPS):
            mid = 0.5 * lo + 0.5 * hi
            c = count(lambda x, cc: x >= mid)
            live = done < 0.5
            up = jnp.logical_and(live, c >= kf)
            down = jnp.logical_and(live, c < kf)
            lo = jnp.where(up, mid, lo)
            c_lo = jnp.where(up, c, c_lo)
            hi = jnp.where(down, mid, hi)
            c_hi = jnp.where(down, c, c_hi)
            done = finished(c_lo, c_hi, done)
        return lo, hi, c_lo, c_hi, snap, done

    col = lambda v: jnp.full(n_valid.shape, v, F32)
    init = (row_min, col(pos), n_valid.astype(F32), col(0.0), col(0.0), jnp.where(small, 1.0, 0.0).astype(F32))
    lo, hi, c_lo, c_hi, snap, _ = lax.while_loop(cond, body, init)
    from_hi = jnp.logical_or(snap > 0.5, c_hi == kf - 1.0)
    above_lo = reduce_all(lambda x, c: jnp.where(x >= lo, x, pos), jnp.minimum, pos, jnp.min)
    thr = jnp.where(small, neg, jnp.where(from_hi, max_below(hi), above_lo))

    need = kf - count(lambda x, c: x > thr)
    n_eq = count(lambda x, c: x == thr)
    tie = jnp.logical_and(jnp.logical_not(small), n_eq > need)
    total = score_ref.shape[0] * width
    col0 = lax.broadcasted_iota(jnp.int32, (1, width) if key_axis == 1 else (width, 1), key_axis)

    def tie_search():
        lo = col(-1.0)
        hi = col(float(total - 1))
        for _ in range(int(np.ceil(np.log2(total))) + 1):
            mid = jnp.floor(0.5 * (lo + hi))
            ok = count(lambda x, c: jnp.logical_and(x == thr, (col0 + c * width).astype(F32) <= mid)) >= need
            hi = jnp.where(ok, mid, hi)
            lo = jnp.where(ok, lo, mid)
        return hi

    any_tie = jnp.sum(jnp.where(tie, 1.0, 0.0)) > 0.0
    jthr = lax.cond(any_tie, tie_search, lambda: col(float(total)))
    jthr = jnp.where(small, -1.0, jthr)
    return thr, jthr


ACC_W = HEAD_DIM + BF16_ROWS


def _with_ones(vt):
    return jnp.concatenate([vt, jnp.ones((BF16_ROWS, vt.shape[1]), vt.dtype)], axis=0)


def _logits_step(c, h, s, s_ref, mx_ref):
    s_ref[c, h] = s
    mx_ref[h] = jnp.maximum(mx_ref[h], _fold_lanes(s, jnp.maximum))


def _row_max(mx_ref, m_ref):
    for h in range(N_HEADS):
        m_ref[h] = jnp.max(mx_ref[h], axis=1, keepdims=True)


def _values_step(c, h, vt_ones, s_ref, m_ref, acc_ref):
    p = jnp.exp(s_ref[c, h] - m_ref[h])
    acc_ref[h] += lax.dot_general(p.astype(BF16), vt_ones, _NT, preferred_element_type=F32)


def _softmax_finish(o_ref, acc_ref):
    outs = []
    for h in range(N_HEADS):
        acc = acc_ref[h]
        outs.append(acc[:, :HEAD_DIM] / acc[:, HEAD_DIM:HEAD_DIM + 1])
    o_ref[...] = jnp.concatenate(outs, axis=1).astype(o_ref.dtype)


def _dsa_prompt_kernel(qa_ref, qi_ref, wi_ref, kit_ref, kvt_ref, o_ref,
                       score_ref, scoret_ref, qs_ref, s_ref, mx_ref, m_ref, acc_ref, *, k_sel):
    i = pl.program_id(1)
    _, tq, kc = score_ref.shape
    n_kc = (i * tq + tq + kc - 1) // kc
    rows = i * tq + lax.broadcasted_iota(jnp.int32, (tq, 1), 0)
    w = wi_ref[:, :IDX_HEADS]
    col0 = lax.broadcasted_iota(jnp.int32, (1, kc), 1)

    def idx_body(c, carry):
        k0 = pl.multiple_of(c * kc, kc)
        kblk = kit_ref[0, :, pl.ds(k0, kc)]
        acc = jnp.zeros((tq, kc), F32)
        for h in range(IDX_HEADS):
            s = jnp.dot(qi_ref[:, h * IDX_DIM:(h + 1) * IDX_DIM], kblk, preferred_element_type=F32)
            acc = acc + jnp.maximum(s, 0.0) * w[:, h:h + 1]
        masked = jnp.where(col0 + k0 <= rows, acc, -jnp.inf)
        score_ref[c] = masked
        scoret_ref[c] = masked.T
        return carry

    lax.fori_loop(0, n_kc, idx_body, 0)
    n_valid = i * tq + 1 + lax.broadcasted_iota(jnp.int32, (1, tq), 1)
    thr_t, jthr_t = _select_threshold(scoret_ref, n_kc, n_valid, k_sel, 0)
    to_column = lambda v: jnp.broadcast_to(v, (LANES, tq)).T[:, :1]
    thr, jthr = to_column(thr_t), to_column(jthr_t)

    qs_ref[...] = qa_ref[...] * SM_SCALE
    mx_ref[...] = jnp.full(mx_ref.shape, MASKED, F32)
    acc_ref[...] = jnp.zeros(acc_ref.shape, F32)

    def logits_body(c, carry):
        k0 = pl.multiple_of(c * kc, kc)
        x = score_ref[c]
        sel = jnp.logical_or(x > thr, jnp.logical_and(x == thr, (col0 + k0).astype(F32) <= jthr))
        bias = jnp.where(sel, 0.0, MASKED)
        for g in range(N_KV):
            kt = kvt_ref[0, 2 * g, :, pl.ds(k0, kc)]
            for r in range(GROUP):
                h = g * GROUP + r
                s = jnp.dot(qs_ref[:, h * HEAD_DIM:(h + 1) * HEAD_DIM], kt, preferred_element_type=F32) + bias
                _logits_step(c, h, s, s_ref, mx_ref)
        return carry

    lax.fori_loop(0, n_kc, logits_body, 0)
    _row_max(mx_ref, m_ref)

    def values_body(c, carry):
        k0 = pl.multiple_of(c * kc, kc)
        for g in range(N_KV):
            vt = _with_ones(kvt_ref[0, 2 * g + 1, :, pl.ds(k0, kc)])
            for r in range(GROUP):
                _values_step(c, g * GROUP + r, vt, s_ref, m_ref, acc_ref)
        return carry

    lax.fori_loop(0, n_kc, values_body, 0)
    _softmax_finish(o_ref, acc_ref)


def _dsa_prompt(pp, b, t, tq, kc):
    n_t = t // tq
    k_sel = min(IDX_TOPK, t // 4)
    kern = functools.partial(_dsa_prompt_kernel, k_sel=k_sel)
    return pl.pallas_call(
        kern, grid=(b, n_t),
        in_specs=[pl.BlockSpec((tq, WIDTH), lambda bb, i: (bb * n_t + i, 0)),
                  pl.BlockSpec((tq, WIDTH), lambda bb, i: (bb * n_t + i, 1)),
                  pl.BlockSpec((tq, LANES), lambda bb, i: (bb * n_t + i, 0)),
                  pl.BlockSpec((1, IDX_DIM, t), lambda bb, i: (bb, 0, 0)),
                  pl.BlockSpec((1, KV_SLABS, HEAD_DIM, t), lambda bb, i: (bb, 0, 0, 0))],
        out_specs=pl.BlockSpec((tq, WIDTH), lambda bb, i: (bb * n_t + i, 0)),
        out_shape=jax.ShapeDtypeStruct((b * t, WIDTH), BF16),
        scratch_shapes=[pltpu.VMEM((t // kc, tq, kc), F32), pltpu.VMEM((t // kc, kc, tq), F32),
                        pltpu.VMEM((tq, WIDTH), BF16),
                        pltpu.VMEM((t // kc, N_HEADS, tq, kc), F32), pltpu.VMEM((N_HEADS, tq, LANES), F32),
                        pltpu.VMEM((N_HEADS, tq, 1), F32), pltpu.VMEM((N_HEADS, tq, ACC_W), F32)],
        compiler_params=_cparams("arbitrary", "arbitrary"), name="dsa_prompt",
    )(pp["q16"], pp["q16"], pp["wi"], pp["kit16"], pp["kvat16"])


def _top_blocks(gate, valid, axis):
    idx = lax.broadcasted_iota(jnp.int32, gate.shape, axis).astype(F32)
    g = jnp.where(valid, gate, -jnp.inf)
    sel = jnp.zeros(gate.shape, F32)
    for _ in range(MOBA_TOPK):
        m = jnp.max(g, axis=axis, keepdims=True)
        first = jnp.min(jnp.where(g == m, idx, float(gate.shape[axis])), axis=axis, keepdims=True)
        pick = idx == first
        sel = jnp.where(jnp.logical_and(pick, m > -jnp.inf), 1.0, sel)
        g = jnp.where(pick, -jnp.inf, g)
    return sel


def _block_indicator(n_rows, n_cols, scale):
    blk = lax.broadcasted_iota(jnp.int32, (n_rows, n_cols), 1) // MOBA_BLOCK
    return jnp.where(blk == lax.broadcasted_iota(jnp.int32, (n_rows, n_cols), 0), scale, 0.0).astype(BF16)


def _moba_prompt_kernel(qb_ref, kvt_ref, o_ref, kext_ref, kmean_ref, qe_ref, s_ref, mx_ref, m_ref, acc_ref):
    own = pl.program_id(1)
    t = kvt_ref.shape[3]
    tq = qb_ref.shape[0]
    n_ext = kext_ref.shape[1] - HEAD_DIM

    @pl.when(own == 0)
    def _():
        ind = _block_indicator(n_ext, t, 1.0)
        for g in range(N_KV):
            kt = kvt_ref[0, 2 * g]
            kext_ref[g, :HEAD_DIM, :] = kt
            kext_ref[g, HEAD_DIM:, :] = ind
            kmean_ref[g] = lax.dot_general(ind, kt, _NT, preferred_element_type=F32) * (1.0 / MOBA_BLOCK)

    blk = lax.broadcasted_iota(jnp.int32, (n_ext, tq), 0)
    for h in range(N_HEADS):
        g = h // GROUP
        q = qb_ref[:, h * HEAD_DIM:(h + 1) * HEAD_DIM]
        gate_t = lax.dot_general(kmean_ref[g].astype(BF16), q, _NT, preferred_element_type=F32)
        picked_t = _top_blocks(gate_t, blk < own, 0)
        allowed = jnp.logical_or(picked_t > 0.5, blk == own)
        bias = _pad_rows(jnp.where(allowed, 0.0, MASKED), LANES).T[:, :n_ext]
        qe_ref[h] = jnp.concatenate([q * SM_SCALE, bias.astype(BF16)], axis=1)

    mx_ref[...] = jnp.full(mx_ref.shape, MASKED, F32)
    acc_ref[...] = jnp.zeros(acc_ref.shape, F32)

    def logits(n, extra_bias):
        k0 = pl.multiple_of(n * MOBA_BLOCK, MOBA_BLOCK)
        for g in range(N_KV):
            ke = kext_ref[g, :, pl.ds(k0, MOBA_BLOCK)]
            for r in range(GROUP):
                h = g * GROUP + r
                s = jnp.dot(qe_ref[h], ke, preferred_element_type=F32)
                _logits_step(n, h, s if extra_bias is None else s + extra_bias, s_ref, mx_ref)

    def logits_body(n, carry):
        logits(n, None)
        return carry

    lax.fori_loop(0, own, logits_body, 0)
    causal = (lax.broadcasted_iota(jnp.int32, (tq, MOBA_BLOCK), 1)
              <= lax.broadcasted_iota(jnp.int32, (tq, MOBA_BLOCK), 0))
    logits(own, jnp.where(causal, 0.0, MASKED))
    _row_max(mx_ref, m_ref)

    def values_body(n, carry):
        k0 = pl.multiple_of(n * MOBA_BLOCK, MOBA_BLOCK)
        for g in range(N_KV):
            vt = _with_ones(kvt_ref[0, 2 * g + 1, :, pl.ds(k0, MOBA_BLOCK)])
            for r in range(GROUP):
                _values_step(n, g * GROUP + r, vt, s_ref, m_ref, acc_ref)
        return carry

    lax.fori_loop(0, own + 1, values_body, 0)
    _softmax_finish(o_ref, acc_ref)


def _moba_prompt(pp, b, t):
    tq = MOBA_BLOCK
    n_t = t // tq
    assert t % MOBA_BLOCK == 0 and n_t <= LANES
    n_ext = -(-n_t // BF16_ROWS) * BF16_ROWS
    return pl.pallas_call(
        _moba_prompt_kernel, grid=(b, n_t),
        in_specs=[pl.BlockSpec((tq, WIDTH), lambda bb, i: (bb * n_t + i, 2)),
                  pl.BlockSpec((1, KV_SLABS, HEAD_DIM, t), lambda bb, i: (bb, 0, 0, 0))],
        out_specs=pl.BlockSpec((tq, WIDTH), lambda bb, i: (bb * n_t + i, 0)),
        out_shape=jax.ShapeDtypeStruct((b * t, WIDTH), BF16),
        scratch_shapes=[pltpu.VMEM((N_KV, HEAD_DIM + n_ext, t), BF16), pltpu.VMEM((N_KV, n_ext, HEAD_DIM), F32),
                        pltpu.VMEM((N_HEADS, tq, HEAD_DIM + n_ext), BF16),
                        pltpu.VMEM((n_t, N_HEADS, tq, MOBA_BLOCK), F32), pltpu.VMEM((N_HEADS, tq, LANES), F32),
                        pltpu.VMEM((N_HEADS, tq, 1), F32), pltpu.VMEM((N_HEADS, tq, ACC_W), F32)],
        compiler_params=_cparams("arbitrary", "arbitrary"), name="moba_prompt",
    )(pp["q16"], pp["kvbt16"])


SEQ_PER_STEP = 2
DECODE_CHUNK = 2048
SELECT_SEQ_PER_STEP = 8


def _page_copy(pt_ref, pool, buf, sems, j, seq, slot, u, p):
    off = pl.multiple_of(p * PAGE_SIZE, PAGE_SIZE)
    lead = (slice(None),) * (len(buf.shape) - 3)
    return pltpu.make_async_copy(pool.at[pt_ref[seq, p]], buf.at[(slot, u) + lead + (pl.ds(off, PAGE_SIZE),)],
                                 sems.at[2 * j + slot])


def _paged_fetch(pt_ref, pools, bufs, sems, past):
    b = pl.program_id(0)
    nb = pl.num_programs(0)
    slot = b % 2
    n_pages = past // PAGE_SIZE
    n_seq = bufs[0].shape[1]

    def each_copy(step, sl, p, fn):
        for u in range(n_seq):
            for j, (pool, buf) in enumerate(zip(pools, bufs)):
                fn(_page_copy(pt_ref, pool, buf, sems, j, step * n_seq + u, sl, u, p))

    def start(step, sl):
        def body(p, carry):
            each_copy(step, sl, p, lambda cp: cp.start())
            return carry
        lax.fori_loop(0, n_pages, body, 0)

    @pl.when(b == 0)
    def _():
        start(0, 0)

    @pl.when(b + 1 < nb)
    def _():
        start(b + 1, 1 - slot)

    def wait_body(p, carry):
        each_copy(b, slot, p, lambda cp: cp.wait())
        return carry
    lax.fori_loop(0, n_pages, wait_body, 0)
    return slot


def _stack_heads(q, heads, scale=1.0):
    qf = q.astype(F32) * scale
    return jnp.concatenate([qf[:, h * HEAD_DIM:(h + 1) * HEAD_DIM] for h in heads], axis=0).astype(BF16)


def _pad_rows(x, n):
    return jnp.concatenate([x, jnp.zeros((n - x.shape[0], x.shape[1]), x.dtype)], axis=0)


def _chunked_attention(queries, kt_of, vt_of, bias_of, bias_new, k_new, v_new, s_ref, chunk):
    n = len(queries)
    rows = queries[0].shape[0]
    n_chunks = s_ref.shape[2] // chunk

    def logits_body(c, mx):
        k0 = pl.multiple_of(c * chunk, chunk)
        out = []
        for i in range(n):
            s = jnp.dot(queries[i], kt_of(i, k0), preferred_element_type=F32) + bias_of(i, k0)
            s_ref[i, :, pl.ds(k0, chunk)] = s
            out.append(jnp.maximum(mx[i], _fold_lanes(s, jnp.maximum)))
        return tuple(out)

    mx = lax.fori_loop(0, n_chunks, logits_body, tuple(jnp.full((rows, LANES), MASKED, F32) for _ in range(n)))
    s_new = [lax.dot_general(queries[i], k_new[i], _NT, preferred_element_type=F32) + bias_new[i] for i in range(n)]
    m = [jnp.maximum(jnp.max(mx[i], axis=1, keepdims=True), jnp.max(s_new[i], axis=1, keepdims=True))
         for i in range(n)]

    def values_body(c, carry):
        k0 = pl.multiple_of(c * chunk, chunk)
        out = []
        for i in range(n):
            l, acc = carry[i]
            p = jnp.exp(s_ref[i, :, pl.ds(k0, chunk)] - m[i])
            out.append((l + _fold_lanes(p, jnp.add),
                        acc + lax.dot_general(p.astype(BF16), vt_of(i, k0), _NT, preferred_element_type=F32)))
        return tuple(out)

    init = tuple((jnp.zeros((rows, LANES), F32), jnp.zeros((rows, HEAD_DIM), F32)) for _ in range(n))
    carry = lax.fori_loop(0, n_chunks, values_body, init)
    outs = []
    for i in range(n):
        l, acc = carry[i]
        p_new = jnp.exp(s_new[i] - m[i])
        l = jnp.sum(l, axis=1, keepdims=True) + jnp.sum(p_new, axis=1, keepdims=True)
        acc = acc + jnp.dot(p_new.astype(BF16), v_new[i], preferred_element_type=F32)
        outs.append(acc / l)
    return outs


def _store_heads(o_ref, u, per_group, nq):
    outs = [per_group[g][r * nq:(r + 1) * nq, :] for g in range(N_KV) for r in range(GROUP)]
    o_ref[u] = jnp.concatenate(outs, axis=1).astype(o_ref.dtype)


def _new_kv(kv_new, g):
    k = kv_new[:, 2 * g * HEAD_DIM:(2 * g + 1) * HEAD_DIM]
    v = kv_new[:, (2 * g + 1) * HEAD_DIM:(2 * g + 2) * HEAD_DIM]
    return _pad_rows(k, LANES).astype(BF16), _pad_rows(v, LANES).astype(BF16)


def _units():
    return [(u, g) for u in range(SEQ_PER_STEP) for g in range(N_KV)]


def _dsa_select_kernel(pt_ref, q_ref, kiw_ref, idx_pool, bias_ref, idxbuf, score_ref, sems, *, past, k_sel, chunk):
    n_seq, nq = q_ref.shape[0], q_ref.shape[1]
    slot = _paged_fetch(pt_ref, (idx_pool,), (idxbuf,), sems, past)
    col_new = lax.broadcasted_iota(jnp.int32, (1, LANES), 1)
    qidx = lax.broadcasted_iota(jnp.int32, (nq, 1), 0)
    qi_rows = [_stack_heads(q_ref[u], range(IDX_HEADS)) for u in range(n_seq)]
    kiw = [kiw_ref[u] for u in range(n_seq)]

    def head_sum(s, u):
        score = jnp.zeros((nq, s.shape[1]), F32)
        for h in range(IDX_HEADS):
            score = score + jnp.maximum(s[h * nq:(h + 1) * nq, :], 0.0) * kiw[u][:, IDX_DIM + h:IDX_DIM + h + 1]
        return score

    def idx_body(c, carry):
        k0 = pl.multiple_of(c * chunk, chunk)
        for u in range(n_seq):
            keys = idxbuf[slot, u, :, pl.ds(k0, chunk)].astype(BF16)
            score_ref[0, u * nq:(u + 1) * nq, pl.ds(k0, chunk)] = head_sum(
                jnp.dot(qi_rows[u], keys, preferred_element_type=F32), u)
        return carry

    lax.fori_loop(0, past // chunk, idx_body, 0)
    for u in range(n_seq):
        k_new = _pad_rows(kiw[u][:, :IDX_DIM], LANES).astype(BF16)
        s = head_sum(lax.dot_general(qi_rows[u], k_new, _NT, preferred_element_type=F32), u)
        score_ref[0, u * nq:(u + 1) * nq, past:] = jnp.where(col_new <= qidx, s, -jnp.inf)

    n_valid = jnp.concatenate([past + qidx + 1] * n_seq, axis=0)
    thr, jthr = _select_threshold(score_ref, 1, n_valid, k_sel, 1)
    x = score_ref[0]
    col = lax.broadcasted_iota(jnp.int32, (1, x.shape[1]), 1).astype(F32)
    sel = jnp.logical_or(x > thr, jnp.logical_and(x == thr, col <= jthr))
    bias = jnp.where(sel, 0.0, MASKED)
    for u in range(n_seq):
        bias_ref[u] = bias[u * nq:(u + 1) * nq, :]


def _dsa_decode_kernel(pt_ref, q_ref, kva_ref, bias_ref, kv_pool, o_ref, kvbuf, s_ref, sems, *, past, chunk):
    nq = q_ref.shape[1]
    slot = _paged_fetch(pt_ref, (kv_pool,), (kvbuf,), sems, past)

    def bias_rows(u, k0, width):
        return jnp.concatenate([bias_ref[u, :, pl.ds(k0, width)]] * GROUP, axis=0)

    units = _units()
    queries = [_stack_heads(q_ref[u], range(g * GROUP, (g + 1) * GROUP), SM_SCALE) for u, g in units]
    new = [_new_kv(kva_ref[u], g) for u, g in units]
    outs = _chunked_attention(
        queries,
        lambda i, k0: kvbuf[slot, units[i][0], 2 * units[i][1], :, pl.ds(k0, chunk)].astype(BF16),
        lambda i, k0: kvbuf[slot, units[i][0], 2 * units[i][1] + 1, :, pl.ds(k0, chunk)].astype(BF16),
        lambda i, k0: bias_rows(units[i][0], k0, chunk),
        [bias_rows(u, past, LANES) for u, g in units], [kv[0] for kv in new], [kv[1] for kv in new],
        s_ref, chunk)
    for u in range(SEQ_PER_STEP):
        _store_heads(o_ref, u, outs[u * N_KV:(u + 1) * N_KV], nq)


def _moba_decode_kernel(pt_ref, q_ref, kvb_ref, kv_pool, o_ref, kvbuf, s_ref, sems, *, past, chunk):
    nq = q_ref.shape[1]
    n_blk = past // MOBA_BLOCK
    n_chunks = past // chunk
    slot = _paged_fetch(pt_ref, (kv_pool,), (kvbuf,), sems, past)
    rows = GROUP * nq
    units = _units()
    kt_of = lambda i, k0: kvbuf[slot, units[i][0], 2 * units[i][1], :, pl.ds(k0, chunk)].astype(BF16)

    def indicator(k0):
        blk = (k0 + lax.broadcasted_iota(jnp.int32, (n_blk, chunk), 1)) // MOBA_BLOCK
        return jnp.where(blk == lax.broadcasted_iota(jnp.int32, (n_blk, chunk), 0), 1.0, 0.0).astype(BF16)

    def mean_body(c, sums):
        k0 = pl.multiple_of(c * chunk, chunk)
        ind = indicator(k0)
        return tuple(sums[i] + lax.dot_general(ind, kt_of(i, k0), _NT, preferred_element_type=F32)
                     for i in range(len(units)))

    sums = lax.fori_loop(0, n_chunks, mean_body, tuple(jnp.zeros((n_blk, HEAD_DIM), F32) for _ in units))

    queries, bias_blk = [], []
    for i, (u, g) in enumerate(units):
        qb = q_ref[u]
        kmean = (sums[i] * (1.0 / MOBA_BLOCK)).astype(BF16)
        gate = lax.dot_general(_stack_heads(qb, range(g * GROUP, (g + 1) * GROUP)), kmean, _NT,
                               preferred_element_type=F32)
        picked = _top_blocks(gate, jnp.full(gate.shape, True), 1)
        bias_blk.append(jnp.where(picked > 0.5, 0.0, MASKED).astype(BF16))
        queries.append(_stack_heads(qb, range(g * GROUP, (g + 1) * GROUP), SM_SCALE))

    col = lax.broadcasted_iota(jnp.int32, (rows, LANES), 1)
    qidx = lax.broadcasted_iota(jnp.int32, (rows, LANES), 0) % nq
    bias_new = jnp.where(col <= qidx, 0.0, MASKED)
    new = [_new_kv(kvb_ref[u], g) for u, g in units]
    outs = _chunked_attention(
        queries, kt_of,
        lambda i, k0: kvbuf[slot, units[i][0], 2 * units[i][1] + 1, :, pl.ds(k0, chunk)].astype(BF16),
        lambda i, k0: jnp.dot(bias_blk[i], indicator(k0), preferred_element_type=F32),
        [bias_new] * len(units), [kv[0] for kv in new], [kv[1] for kv in new], s_ref, chunk)
    for u in range(SEQ_PER_STEP):
        _store_heads(o_ref, u, outs[u * N_KV:(u + 1) * N_KV], nq)


def _sample_specs(n_seq, nq, widths_blocks):
    return [pl.BlockSpec((n_seq, nq, w), lambda bb, pt, j=j: (bb, 0, j)) for w, j in widths_blocks]


def _dsa_decode(ps, page_table, idx_pool_t, kv_pool_t, db, nq):
    past = page_table.shape[1] * PAGE_SIZE
    lp = past + LANES
    k_sel = min(IDX_TOPK, (past + nq) // 4)
    chunk = math.gcd(past, DECODE_CHUNK)
    q16 = ps["q16"].reshape(db, nq, -1)
    n_sel = math.gcd(db, SELECT_SEQ_PER_STEP)
    assert db % SEQ_PER_STEP == 0

    bias = pl.pallas_call(
        functools.partial(_dsa_select_kernel, past=past, k_sel=k_sel, chunk=chunk),
        grid_spec=pltpu.PrefetchScalarGridSpec(
            num_scalar_prefetch=1, grid=(db // n_sel,),
            in_specs=_sample_specs(n_sel, nq, [(WIDTH, 1), (LANES, 0)]) + [pl.BlockSpec(memory_space=pl.ANY)],
            out_specs=_sample_specs(n_sel, nq, [(lp, 0)])[0],
            scratch_shapes=[pltpu.VMEM((2, n_sel, IDX_DIM, past), F32), pltpu.VMEM((1, n_sel * nq, lp), F32),
                            pltpu.SemaphoreType.DMA((2,))]),
        out_shape=jax.ShapeDtypeStruct((db, nq, lp), F32),
        compiler_params=_cparams("arbitrary"), name="dsa_select",
    )(page_table, q16, ps["kiw"].reshape(db, nq, LANES), idx_pool_t)

    return pl.pallas_call(
        functools.partial(_dsa_decode_kernel, past=past, chunk=chunk),
        grid_spec=pltpu.PrefetchScalarGridSpec(
            num_scalar_prefetch=1, grid=(db // SEQ_PER_STEP,),
            in_specs=_sample_specs(SEQ_PER_STEP, nq, [(WIDTH, 0), (KV_WIDTH, 0), (lp, 0)])
            + [pl.BlockSpec(memory_space=pl.ANY)],
            out_specs=_sample_specs(SEQ_PER_STEP, nq, [(WIDTH, 0)])[0],
            scratch_shapes=[pltpu.VMEM((2, SEQ_PER_STEP, KV_SLABS, HEAD_DIM, past), F32),
                            pltpu.VMEM((SEQ_PER_STEP * N_KV, GROUP * nq, past), F32),
                            pltpu.SemaphoreType.DMA((2,))]),
        out_shape=jax.ShapeDtypeStruct((db, nq, WIDTH), BF16),
        compiler_params=_cparams("arbitrary"), name="dsa_decode",
    )(page_table, q16, ps["kva"].reshape(db, nq, KV_WIDTH), bias, kv_pool_t)


def _moba_decode(ps, page_table, kv_pool_t, db, nq):
    past = page_table.shape[1] * PAGE_SIZE
    assert past % MOBA_BLOCK == 0 and nq <= MOBA_BLOCK and db % SEQ_PER_STEP == 0
    qw = ps["q16"].shape[1]
    kern = functools.partial(_moba_decode_kernel, past=past, chunk=math.gcd(past, DECODE_CHUNK))
    grid_spec = pltpu.PrefetchScalarGridSpec(
        num_scalar_prefetch=1, grid=(db // SEQ_PER_STEP,),
        in_specs=_sample_specs(SEQ_PER_STEP, nq, [(WIDTH, 2), (KV_WIDTH, 0)]) + [pl.BlockSpec(memory_space=pl.ANY)],
        out_specs=_sample_specs(SEQ_PER_STEP, nq, [(WIDTH, 0)])[0],
        scratch_shapes=[pltpu.VMEM((2, SEQ_PER_STEP, KV_SLABS, HEAD_DIM, past), F32),
                        pltpu.VMEM((SEQ_PER_STEP * N_KV, GROUP * nq, past), F32), pltpu.SemaphoreType.DMA((2,))])
    return pl.pallas_call(
        kern, grid_spec=grid_spec, out_shape=jax.ShapeDtypeStruct((db, nq, WIDTH), BF16),
        compiler_params=_cparams("arbitrary"), name="moba_decode",
    )(page_table, ps["q16"].reshape(db, nq, qw), ps["kvb"].reshape(db, nq, KV_WIDTH), kv_pool_t)


def _out_kernel(x_ref, oa_ref, ob_ref, gate_ref, wba_ref, wbb_ref, wo_ref, y_ref):
    d = x_ref.shape[1]
    ua = oa_ref[...] * gate_ref[:, 0:WIDTH]
    ub = ob_ref[...] * gate_ref[:, WIDTH:2 * WIDTH]
    a = jnp.dot(ua, wba_ref[...], preferred_element_type=F32)
    b = jnp.dot(ub, wbb_ref[...], preferred_element_type=F32)
    ga = gate_ref[:, 2 * WIDTH:2 * WIDTH + d].astype(F32)
    gb = gate_ref[:, 2 * WIDTH + d:2 * WIDTH + 2 * d].astype(F32)
    merged = (ga * a + gb * b).astype(BF16)
    y_ref[...] = x_ref[...] + jnp.dot(merged, wo_ref[...], preferred_element_type=F32)


def _out_proj(x2d, oa, ob, gate16, wts, tm):
    n, d = x2d.shape
    row = lambda w: pl.BlockSpec((tm, w), lambda i: (i, 0))
    return pl.pallas_call(
        _out_kernel, grid=(n // tm,),
        in_specs=[row(d), row(WIDTH), row(WIDTH), row(gate16.shape[1]),
                  _const_spec((WIDTH, d)), _const_spec((WIDTH, d)), _const_spec((d, d))],
        out_specs=row(d), out_shape=jax.ShapeDtypeStruct((n, d), F32),
        compiler_params=_cparams("arbitrary"), name="out_proj",
    )(x2d, oa, ob, gate16, wts["wba"], wts["wbb"], wts["wo"])


def _rope_angles(pos):
    inv = ROPE_THETA ** (-jnp.arange(ROT_HALF, dtype=F32) / ROT_HALF)
    ang = pos.astype(F32)[:, None] * inv[None, :]
    return jnp.cos(ang), jnp.sin(ang)


def _rope_tables(pos):
    p = pos.shape[0]
    c, s = _rope_angles(pos)
    rest = HEAD_DIM - 2 * ROT_HALF
    z8, zr = jnp.zeros((p, ROT_HALF), F32), jnp.zeros((p, rest), F32)
    c_head = jnp.concatenate([c, c, jnp.ones((p, rest), F32)], axis=1)
    s1_head = jnp.concatenate([-s, z8, zr], axis=1)
    s2_head = jnp.concatenate([z8, s, zr], axis=1)
    one, zero = jnp.ones((p, HEAD_DIM), F32), jnp.zeros((p, HEAD_DIM), F32)
    cat = lambda a, b: jnp.concatenate([a, b], axis=1)
    return (cat(c_head, c_head), cat(s1_head, s1_head), cat(s2_head, s2_head),
            cat(c_head, one), cat(s1_head, zero), cat(s2_head, zero))


def _prep_weights(norm_g, w_in, qn_a, kn_a, qn_i, kn_i, qn_b, kn_b, w_ba, w_bb, w_out):
    d = w_in.shape[0]
    kvw = N_KV * HEAD_DIM
    splits = (WIDTH, kvw, kvw, IDX_HEADS * IDX_DIM, IDX_DIM, IDX_HEADS, WIDTH, WIDTH, kvw, kvw, WIDTH, d, d)
    offs = np.concatenate([[0], np.cumsum(splits)])
    qa, ka, va, qi, ki, wi, za, qb, kb, vb, zb, ga, gb = [w_in[:, offs[j]:offs[j + 1]] for j in range(13)]
    hd = HEAD_DIM
    pad = jnp.zeros((d, LANES - IDX_DIM - IDX_HEADS), F32)
    slabs = jnp.concatenate([ka[:, :hd], va[:, :hd], ka[:, hd:], va[:, hd:],
                             kb[:, :hd], vb[:, :hd], kb[:, hd:], vb[:, hd:], ki], axis=1)
    wk = jnp.concatenate([slabs, wi, pad], axis=1)
    wwi = jnp.concatenate([wi, jnp.zeros((d, LANES - IDX_HEADS), F32)], axis=1)
    zero = jnp.zeros((hd,), F32)
    one = jnp.ones((hd,), F32)
    wi_gain = jnp.concatenate([jnp.full((IDX_HEADS,), (IDX_HEADS * IDX_DIM) ** -0.5, F32),
                               jnp.zeros((hd - IDX_HEADS,), F32)])
    gka = jnp.concatenate([kn_a, zero, kn_a, zero, kn_b, zero, kn_b, zero, kn_i, zero])[None, :]
    gkb = jnp.concatenate([zero, one, zero, one, zero, one, zero, one, zero, wi_gain])[None, :]
    gkt = jnp.concatenate([kn_a, one, kn_a, one, kn_b, one, kn_b, one, kn_i])[:, None]
    gq = jnp.concatenate([jnp.tile(qn_a, N_HEADS), jnp.tile(qn_i, IDX_HEADS), jnp.tile(qn_b, N_HEADS)])[None, :]
    blk = np.arange(MXU_DIM) // HEAD_DIM
    bd = jnp.asarray((blk[:, None] == blk[None, :]) / HEAD_DIM, BF16)
    return dict(
        ng=norm_g[None, :].astype(F32),
        wq=jnp.concatenate([qa, qi, qb], axis=1).astype(BF16), wk=wk.astype(BF16),
        wkt=slabs.T.astype(BF16), wwi=wwi.astype(BF16),
        wg=jnp.concatenate([za, zb, ga, gb], axis=1).astype(BF16),
        gq=gq, gka=gka, gkb=gkb, gkt=gkt, bd=bd,
        wba=w_ba.astype(BF16), wbb=w_bb.astype(BF16), wo=w_out.astype(BF16))


def kernel(x_prompt, x_sample, cache_kv_a, cache_idx_k, cache_kv_b, page_table, norm_g, w_in, q_norm_a, k_norm_a,
           idx_q_norm, idx_k_norm, q_norm_b, k_norm_b, w_branch_a, w_branch_b, w_out):
    b, t, d = x_prompt.shape
    db, nq, _ = x_sample.shape
    depth = w_in.shape[0]
    n_phys = cache_kv_a.shape[1]
    past = page_table.shape[1] * PAGE_SIZE
    tm = 512
    tm_s = min(tm, db * nq)
    assert t % tm == 0 and tm_s % nq == 0 and (db * nq) % tm_s == 0

    pos_p = jnp.arange(t, dtype=jnp.int32)
    cos_p, sin_p = _rope_angles(pos_p)
    tab_p = _rope_tables(pos_p)[:3] + (cos_p.T, sin_p.T)
    tab_s = tuple(jnp.tile(a, (tm_s // nq, 1)) for a in _rope_tables(past + jnp.arange(nq, dtype=jnp.int32)))

    kv_t = lambda pool: jnp.transpose(pool, (0, 2, 3, 4, 1)).reshape(n_phys, KV_SLABS, HEAD_DIM, PAGE_SIZE)

    hp = x_prompt.reshape(b * t, d)
    hs = x_sample.reshape(db * nq, d)
    new = [[] for _ in range(6)]
    for l in range(depth):
        wts = _prep_weights(norm_g[l], w_in[l], q_norm_a[l], k_norm_a[l], idx_q_norm[l], idx_k_norm[l],
                            q_norm_b[l], k_norm_b[l], w_branch_a[l], w_branch_b[l], w_out[l])
        pp = _project_prompt(hp, tab_p, wts, b, t, tm)
        oa = _dsa_prompt(pp, b, t, tq=256, kc=min(512, t))
        ob = _moba_prompt(pp, b, t)
        hp = _out_proj(hp, oa, ob, pp["gate16"], wts, tm)

        ps = _project_sample(hs, tab_s, wts, tm_s)
        oa_s = _dsa_decode(ps, page_table, jnp.transpose(cache_idx_k[l], (0, 2, 1)), kv_t(cache_kv_a[l]), db, nq)
        ob_s = _moba_decode(ps, page_table, kv_t(cache_kv_b[l]), db, nq)
        hs = _out_proj(hs, oa_s.reshape(db * nq, WIDTH), ob_s.reshape(db * nq, WIDTH), ps["gate16"], wts, tm_s)

        to_tokens = lambda a: jnp.transpose(a.reshape(b, N_KV, 2, HEAD_DIM, t), (0, 4, 1, 2, 3))
        new[0].append(to_tokens(pp["kvat"]))
        new[1].append(jnp.transpose(pp["kit"], (0, 2, 1)))
        new[2].append(to_tokens(pp["kvbt"]))
        new[3].append(ps["kva"].reshape(db, nq, N_KV, 2, HEAD_DIM))
        new[4].append(ps["kiw"][:, :IDX_DIM].reshape(db, nq, IDX_DIM))
        new[5].append(ps["kvb"].reshape(db, nq, N_KV, 2, HEAD_DIM))
    return (hp.reshape(b, t, d), hs.reshape(db, nq, d)) + tuple(jnp.stack(a, axis=0) for a in new)
```

```python
import functools
import math

import numpy as np
import jax
import jax.numpy as jnp
from jax import lax
from jax.experimental import pallas as pl
from jax.experimental.pallas import tpu as pltpu

F32 = jnp.float32
BF16 = jnp.bfloat16

HEAD_DIM = 64
N_HEADS = 8
N_KV = 2
GROUP = N_HEADS // N_KV
WIDTH = N_HEADS * HEAD_DIM
KV_WIDTH = N_KV * 2 * HEAD_DIM
KV_SLABS = N_KV * 2
IDX_HEADS = 8
IDX_DIM = 64
IDX_TOPK = 256
MOBA_BLOCK = 256
MOBA_TOPK = 3
PAGE_SIZE = 128
ROPE_THETA = 500000.0
ROT_HALF = HEAD_DIM // 4 // 2
RMS_EPS = 1e-6
SM_SCALE = HEAD_DIM ** -0.5

LANES = 128
SUBLANES = 8
MXU_DIM = 256
BF16_ROWS = 16
VMEM_LIMIT_BYTES = 56 * 1024 * 1024

MASKED = -1e30
BISECT_STEPS = 6

_NT = (((1,), (1,)), ((), ()))


def _cparams(*sem):
    return pltpu.CompilerParams(dimension_semantics=sem, vmem_limit_bytes=VMEM_LIMIT_BYTES)


def _normed_input(x_ref, ng_ref):
    x = x_ref[...]
    inv = lax.rsqrt(jnp.mean(x * x, axis=-1, keepdims=True) + RMS_EPS)
    return (x * inv * ng_ref[...]).astype(BF16)


def _head_mean_sq(z, bd_ref):
    parts = []
    for c in range(0, z.shape[1], MXU_DIM):
        cw = min(MXU_DIM, z.shape[1] - c)
        zc = z[:, c:c + cw]
        parts.append(jnp.dot((zc * zc).astype(BF16), bd_ref[:cw, :cw], preferred_element_type=F32))
    return parts[0] if len(parts) == 1 else jnp.concatenate(parts, axis=1)


def _rope_lanes(y, c_ref, s1_ref, s2_ref):
    return (y * c_ref[...] + pltpu.roll(y, LANES - ROT_HALF, 1) * s1_ref[...]
            + pltpu.roll(y, ROT_HALF, 1) * s2_ref[...])


def _queries_and_gates(h, wq_ref, wg_ref, gq_ref, bd_ref, cq_ref, s1q_ref, s2q_ref, q16_ref, gate16_ref):
    for c0 in range(0, q16_ref.shape[1], 512):
        z = jnp.dot(h, wq_ref[:, c0:c0 + 512], preferred_element_type=F32)
        y = z * lax.rsqrt(_head_mean_sq(z, bd_ref) + RMS_EPS) * gq_ref[:, c0:c0 + 512]
        for j in range(0, 512, LANES):
            q16_ref[:, c0 + j:c0 + j + LANES] = _rope_lanes(y[:, j:j + LANES], cq_ref, s1q_ref, s2q_ref).astype(BF16)
    n_silu = 2 * WIDTH
    for c0 in range(0, gate16_ref.shape[1], 512):
        z = jnp.dot(h, wg_ref[:, c0:c0 + 512], preferred_element_type=F32)
        sig = 1.0 / (1.0 + jnp.exp(-z))
        gate16_ref[:, c0:c0 + 512] = (z * sig if c0 < n_silu else sig).astype(BF16)


def _proj_sample_kernel(x_ref, ng_ref, wq_ref, wk_ref, wg_ref, gq_ref, gka_ref, gkb_ref, bd_ref,
                        cq_ref, s1q_ref, s2q_ref, ck_ref, s1k_ref, s2k_ref,
                        q16_ref, gate16_ref, kva_ref, kvb_ref, kiw_ref):
    h = _normed_input(x_ref, ng_ref)
    _queries_and_gates(h, wq_ref, wg_ref, gq_ref, bd_ref, cq_ref, s1q_ref, s2q_ref, q16_ref, gate16_ref)
    z = jnp.dot(h, wk_ref[...], preferred_element_type=F32)
    y = z * (lax.rsqrt(_head_mean_sq(z, bd_ref) + RMS_EPS) * gka_ref[...] + gkb_ref[...])
    r = [_rope_lanes(y[:, j * LANES:(j + 1) * LANES], ck_ref, s1k_ref, s2k_ref) for j in range(5)]
    for j in range(2):
        kva_ref[:, j * LANES:(j + 1) * LANES] = r[j]
        kvb_ref[:, j * LANES:(j + 1) * LANES] = r[2 + j]
    kiw_ref[...] = r[4]


def _proj_prompt_kernel(x_ref, ng_ref, wq_ref, wkt_ref, wwi_ref, wg_ref, gq_ref, gkt_ref, bd_ref,
                        cq_ref, s1q_ref, s2q_ref, ct_ref, st_ref,
                        q16_ref, gate16_ref, wi_ref, kvat_ref, kvbt_ref, kit_ref, kvat16_ref, kvbt16_ref, kit16_ref):
    h = _normed_input(x_ref, ng_ref)
    _queries_and_gates(h, wq_ref, wg_ref, gq_ref, bd_ref, cq_ref, s1q_ref, s2q_ref, q16_ref, gate16_ref)
    wi_ref[...] = jnp.dot(h, wwi_ref[...], preferred_element_type=F32) * (IDX_HEADS * IDX_DIM) ** -0.5

    zt = lax.dot_general(wkt_ref[...], h, _NT, preferred_element_type=F32)
    cos, sin = ct_ref[...], st_ref[...]

    def key_head(j):
        z = zt[j * HEAD_DIM:(j + 1) * HEAD_DIM, :]
        y = z * lax.rsqrt(jnp.mean(z * z, axis=0, keepdims=True) + RMS_EPS) * gkt_ref[j * HEAD_DIM:(j + 1) * HEAD_DIM, :]
        y1, y2 = y[:ROT_HALF], y[ROT_HALF:2 * ROT_HALF]
        return jnp.concatenate([y1 * cos - y2 * sin, y2 * cos + y1 * sin, y[2 * ROT_HALF:]], axis=0)

    for j in range(KV_SLABS):
        for base, out_ref, out16_ref in ((0, kvat_ref, kvat16_ref), (KV_SLABS, kvbt_ref, kvbt16_ref)):
            slab = key_head(base + j) if j % 2 == 0 else zt[(base + j) * HEAD_DIM:(base + j + 1) * HEAD_DIM, :]
            out_ref[0, j] = slab
            out16_ref[0, j] = slab.astype(BF16)
    ki = key_head(2 * KV_SLABS)
    kit_ref[0] = ki
    kit16_ref[0] = ki.astype(BF16)


def _const_spec(shape, n_grid=1):
    if n_grid == 1:
        return pl.BlockSpec(shape, lambda i: (0,) * len(shape))
    return pl.BlockSpec(shape, lambda b, i: (0,) * len(shape))


def _project_sample(x2d, tables, wts, tm):
    n, d = x2d.shape
    wq, wk, wg = wts["wq"], wts["wk"], wts["wg"]
    row = lambda w: pl.BlockSpec((tm, w), lambda i: (i, 0))
    tab = pl.BlockSpec((tm, LANES), lambda i: (0, 0))
    in_specs = [row(d), _const_spec((1, d)), _const_spec(wq.shape), _const_spec(wk.shape), _const_spec(wg.shape),
                _const_spec((1, wq.shape[1])), _const_spec((1, wk.shape[1])), _const_spec((1, wk.shape[1])),
                _const_spec((MXU_DIM, MXU_DIM))] + [tab] * 6
    out_shape = [
        jax.ShapeDtypeStruct((n, wq.shape[1]), BF16),
        jax.ShapeDtypeStruct((n, wg.shape[1]), BF16),
        jax.ShapeDtypeStruct((n, KV_WIDTH), F32),
        jax.ShapeDtypeStruct((n, KV_WIDTH), F32),
        jax.ShapeDtypeStruct((n, LANES), F32),
    ]
    outs = pl.pallas_call(
        _proj_sample_kernel, grid=(n // tm,), in_specs=in_specs, out_specs=[row(s.shape[1]) for s in out_shape],
        out_shape=out_shape, compiler_params=_cparams("arbitrary"), name="project_sample",
    )(x2d, wts["ng"], wq, wk, wg, wts["gq"], wts["gka"], wts["gkb"], wts["bd"], *tables)
    return dict(zip(("q16", "gate16", "kva", "kvb", "kiw"), outs))


def _project_prompt(x2d, tables, wts, b, t, tm):
    n, d = x2d.shape
    n_t = t // tm
    wq, wkt, wwi, wg = wts["wq"], wts["wkt"], wts["wwi"], wts["wg"]
    row = lambda w: pl.BlockSpec((tm, w), lambda bb, i: (bb * n_t + i, 0))
    tab = pl.BlockSpec((tm, LANES), lambda bb, i: (i, 0))
    tab_t = pl.BlockSpec((ROT_HALF, tm), lambda bb, i: (0, i))
    cs = lambda shape: _const_spec(shape, 2)
    in_specs = [row(d), cs((1, d)), cs(wq.shape), cs(wkt.shape), cs(wwi.shape), cs(wg.shape),
                cs((1, wq.shape[1])), cs((wkt.shape[0], 1)), cs((MXU_DIM, MXU_DIM)), tab, tab, tab, tab_t, tab_t]
    kv_t = lambda dt: jax.ShapeDtypeStruct((b, KV_SLABS, HEAD_DIM, t), dt)
    ki_t = lambda dt: jax.ShapeDtypeStruct((b, IDX_DIM, t), dt)
    out_shape = [jax.ShapeDtypeStruct((n, wq.shape[1]), BF16), jax.ShapeDtypeStruct((n, wg.shape[1]), BF16),
                 jax.ShapeDtypeStruct((n, LANES), F32),
                 kv_t(F32), kv_t(F32), ki_t(F32), kv_t(BF16), kv_t(BF16), ki_t(BF16)]
    kv_spec = pl.BlockSpec((1, KV_SLABS, HEAD_DIM, tm), lambda bb, i: (bb, 0, 0, i))
    ki_spec = pl.BlockSpec((1, IDX_DIM, tm), lambda bb, i: (bb, 0, i))
    out_specs = [row(wq.shape[1]), row(wg.shape[1]), row(LANES), kv_spec, kv_spec, ki_spec, kv_spec, kv_spec, ki_spec]
    outs = pl.pallas_call(
        _proj_prompt_kernel, grid=(b, n_t), in_specs=in_specs, out_specs=out_specs, out_shape=out_shape,
        compiler_params=_cparams("arbitrary", "arbitrary"), name="project_prompt",
    )(x2d, wts["ng"], wq, wkt, wwi, wg, wts["gq"], wts["gkt"], wts["bd"], *tables)
    return dict(zip(("q16", "gate16", "wi", "kvat", "kvbt", "kit", "kvat16", "kvbt16", "kit16"), outs))


def _fold_lanes(x, op):
    acc = x[:, :LANES]
    for j in range(1, x.shape[1] // LANES):
        acc = op(acc, x[:, j * LANES:(j + 1) * LANES])
    return acc


def _fold_sublanes(x, op):
    n = x.shape[0] // SUBLANES
    parts = [x[j * SUBLANES:(j + 1) * SUBLANES, :] for j in range(min(n, SUBLANES))]
    for j in range(len(parts), n):
        parts[j % SUBLANES] = op(parts[j % SUBLANES], x[j * SUBLANES:(j + 1) * SUBLANES, :])
    while len(parts) > 1:
        parts = [op(parts[j], parts[j + 1]) for j in range(0, len(parts) - 1, 2)] + parts[len(parts) & ~1:]
    return parts[0]


def _select_threshold(score_ref, n_chunks, n_valid, k, key_axis):
    width = score_ref.shape[1 + key_axis]
    rows = score_ref.shape[2 - key_axis]
    kf = float(k)
    neg, pos = -jnp.inf, jnp.inf
    fold = _fold_lanes if key_axis == 1 else _fold_sublanes
    acc_shape = (rows, LANES) if key_axis == 1 else (SUBLANES, rows)

    def reduce_all(fn, op, init, final_reduce):
        def body(c, acc):
            return op(acc, fold(fn(score_ref[c], c), op))
        acc = lax.fori_loop(0, n_chunks, body, jnp.full(acc_shape, init, F32))
        return final_reduce(acc, axis=key_axis, keepdims=True)

    def count(pred):
        return reduce_all(lambda x, c: jnp.where(pred(x, c), 1.0, 0.0), jnp.add, 0.0, jnp.sum)

    def max_below(v):
        return reduce_all(lambda x, c: jnp.where(x < v, x, neg), jnp.maximum, neg, jnp.max)

    small = n_valid <= k
    row_min = reduce_all(lambda x, c: jnp.where(x > neg, x, pos), jnp.minimum, pos, jnp.min)

    def cond(s):
        return jnp.sum(1.0 - s[5]) > 0.0

    def finished(c_lo, c_hi, done):
        return jnp.where(jnp.logical_or(c_hi == kf - 1.0, c_lo == kf), 1.0, done)

    def body(s):
        lo, hi, c_lo, c_hi, snap, done = s
        vmax = max_below(hi)
        c = count(lambda x, cc: x >= vmax)
        live = done < 0.5
        hit = jnp.logical_and(live, c >= kf)
        move = jnp.logical_and(live, c < kf)
        snap = jnp.where(hit, 1.0, snap)
        hi = jnp.where(move, vmax, hi)
        c_hi = jnp.where(move, c, c_hi)
        done = finished(c_lo, c_hi, jnp.where(hit, 1.0, done))
        for _ in range(BISECT_STEPS):
            mid = 0.5 * lo + 0.5 * hi
            c = count(lambda x, cc: x >= mid)
            live = done < 0.5
            up = jnp.logical_and(live, c >= kf)
            down = jnp.logical_and(live, c < kf)
            lo = jnp.where(up, mid, lo)
            c_lo = jnp.where(up, c, c_lo)
            hi = jnp.where(down, mid, hi)
            c_hi = jnp.where(down, c, c_hi)
            done = finished(c_lo, c_hi, done)
        return lo, hi, c_lo, c_hi, snap, done

    col = lambda v: jnp.full(n_valid.shape, v, F32)
    init = (row_min, col(pos), n_valid.astype(F32), col(0.0), col(0.0), jnp.where(small, 1.0, 0.0).astype(F32))
    lo, hi, c_lo, c_hi, snap, _ = lax.while_loop(cond, body, init)
    from_hi = jnp.logical_or(snap > 0.5, c_hi == kf - 1.0)
    above_lo = reduce_all(lambda x, c: jnp.where(x >= lo, x, pos), jnp.minimum, pos, jnp.min)
    thr = jnp.where(small, neg, jnp.where(from_hi, max_below(hi), above_lo))

    need = kf - count(lambda x, c: x > thr)
    n_eq = count(lambda x, c: x == thr)
    tie = jnp.logical_and(jnp.logical_not(small), n_eq > need)
    total = score_ref.shape[0] * width
    col0 = lax.broadcasted_iota(jnp.int32, (1, width) if key_axis == 1 else (width, 1), key_axis)

    def tie_search():
        lo = col(-1.0)
        hi = col(float(total - 1))
        for _ in range(int(np.ceil(np.log2(total))) + 1):
            mid = jnp.floor(0.5 * (lo + hi))
            ok = count(lambda x, c: jnp.logical_and(x == thr, (col0 + c * width).astype(F32) <= mid)) >= need
            hi = jnp.where(ok, mid, hi)
            lo = jnp.where(ok, lo, mid)
        return hi

    any_tie = jnp.sum(jnp.where(tie, 1.0, 0.0)) > 0.0
    jthr = lax.cond(any_tie, tie_search, lambda: col(float(total)))
    jthr = jnp.where(small, -1.0, jthr)
    return thr, jthr


ACC_W = HEAD_DIM + BF16_ROWS


def _with_ones(vt):
    return jnp.concatenate([vt, jnp.ones((BF16_ROWS, vt.shape[1]), vt.dtype)], axis=0)


def _logits_step(c, h, s, s_ref, mx_ref):
    s_ref[c, h] = s
    mx_ref[h] = jnp.maximum(mx_ref[h], _fold_lanes(s, jnp.maximum))


def _row_max(mx_ref, m_ref):
    for h in range(N_HEADS):
        m_ref[h] = jnp.max(mx_ref[h], axis=1, keepdims=True)


def _values_step(c, h, vt_ones, s_ref, m_ref, acc_ref):
    p = jnp.exp(s_ref[c, h] - m_ref[h])
    acc_ref[h] += lax.dot_general(p.astype(BF16), vt_ones, _NT, preferred_element_type=F32)


def _softmax_finish(o_ref, acc_ref):
    outs = []
    for h in range(N_HEADS):
        acc = acc_ref[h]
        outs.append(acc[:, :HEAD_DIM] / acc[:, HEAD_DIM:HEAD_DIM + 1])
    o_ref[...] = jnp.concatenate(outs, axis=1).astype(o_ref.dtype)


def _dsa_prompt_kernel(qa_ref, qi_ref, wi_ref, kit_ref, kvt_ref, o_ref,
                       score_ref, scoret_ref, qs_ref, s_ref, mx_ref, m_ref, acc_ref, *, k_sel):
    i = pl.program_id(1)
    _, tq, kc = score_ref.shape
    n_kc = (i * tq + tq + kc - 1) // kc
    rows = i * tq + lax.broadcasted_iota(jnp.int32, (tq, 1), 0)
    w = wi_ref[:, :IDX_HEADS]
    col0 = lax.broadcasted_iota(jnp.int32, (1, kc), 1)

    def idx_body(c, carry):
        k0 = pl.multiple_of(c * kc, kc)
        kblk = kit_ref[0, :, pl.ds(k0, kc)]
        acc = jnp.zeros((tq, kc), F32)
        for h in range(IDX_HEADS):
            s = jnp.dot(qi_ref[:, h * IDX_DIM:(h + 1) * IDX_DIM], kblk, preferred_element_type=F32)
            acc = acc + jnp.maximum(s, 0.0) * w[:, h:h + 1]
        masked = jnp.where(col0 + k0 <= rows, acc, -jnp.inf)
        score_ref[c] = masked
        scoret_ref[c] = masked.T
        return carry

    lax.fori_loop(0, n_kc, idx_body, 0)
    n_valid = i * tq + 1 + lax.broadcasted_iota(jnp.int32, (1, tq), 1)
    thr_t, jthr_t = _select_threshold(scoret_ref, n_kc, n_valid, k_sel, 0)
    to_column = lambda v: jnp.broadcast_to(v, (LANES, tq)).T[:, :1]
    thr, jthr = to_column(thr_t), to_column(jthr_t)

    qs_ref[...] = qa_ref[...] * SM_SCALE
    mx_ref[...] = jnp.full(mx_ref.shape, MASKED, F32)
    acc_ref[...] = jnp.zeros(acc_ref.shape, F32)

    def logits_body(c, carry):
        k0 = pl.multiple_of(c * kc, kc)
        x = score_ref[c]
        sel = jnp.logical_or(x > thr, jnp.logical_and(x == thr, (col0 + k0).astype(F32) <= jthr))
        bias = jnp.where(sel, 0.0, MASKED)
        for g in range(N_KV):
            kt = kvt_ref[0, 2 * g, :, pl.ds(k0, kc)]
            for r in range(GROUP):
                h = g * GROUP + r
                s = jnp.dot(qs_ref[:, h * HEAD_DIM:(h + 1) * HEAD_DIM], kt, preferred_element_type=F32) + bias
                _logits_step(c, h, s, s_ref, mx_ref)
        return carry

    lax.fori_loop(0, n_kc, logits_body, 0)
    _row_max(mx_ref, m_ref)

    def values_body(c, carry):
        k0 = pl.multiple_of(c * kc, kc)
        for g in range(N_KV):
            vt = _with_ones(kvt_ref[0, 2 * g + 1, :, pl.ds(k0, kc)])
            for r in range(GROUP):
                _values_step(c, g * GROUP + r, vt, s_ref, m_ref, acc_ref)
        return carry

    lax.fori_loop(0, n_kc, values_body, 0)
    _softmax_finish(o_ref, acc_ref)


def _dsa_prompt(pp, b, t, tq, kc):
    n_t = t // tq
    k_sel = min(IDX_TOPK, t // 4)
    kern = functools.partial(_dsa_prompt_kernel, k_sel=k_sel)
    return pl.pallas_call(
        kern, grid=(b, n_t),
        in_specs=[pl.BlockSpec((tq, WIDTH), lambda bb, i: (bb * n_t + i, 0)),
                  pl.BlockSpec((tq, WIDTH), lambda bb, i: (bb * n_t + i, 1)),
                  pl.BlockSpec((tq, LANES), lambda bb, i: (bb * n_t + i, 0)),
                  pl.BlockSpec((1, IDX_DIM, t), lambda bb, i: (bb, 0, 0)),
                  pl.BlockSpec((1, KV_SLABS, HEAD_DIM, t), lambda bb, i: (bb, 0, 0, 0))],
        out_specs=pl.BlockSpec((tq, WIDTH), lambda bb, i: (bb * n_t + i, 0)),
        out_shape=jax.ShapeDtypeStruct((b * t, WIDTH), BF16),
        scratch_shapes=[pltpu.VMEM((t // kc, tq, kc), F32), pltpu.VMEM((t // kc, kc, tq), F32),
                        pltpu.VMEM((tq, WIDTH), BF16),
                        pltpu.VMEM((t // kc, N_HEADS, tq, kc), F32), pltpu.VMEM((N_HEADS, tq, LANES), F32),
                        pltpu.VMEM((N_HEADS, tq, 1), F32), pltpu.VMEM((N_HEADS, tq, ACC_W), F32)],
        compiler_params=_cparams("arbitrary", "arbitrary"), name="dsa_prompt",
    )(pp["q16"], pp["q16"], pp["wi"], pp["kit16"], pp["kvat16"])


def _top_blocks(gate, valid, axis):
    idx = lax.broadcasted_iota(jnp.int32, gate.shape, axis).astype(F32)
    g = jnp.where(valid, gate, -jnp.inf)
    sel = jnp.zeros(gate.shape, F32)
    for _ in range(MOBA_TOPK):
        m = jnp.max(g, axis=axis, keepdims=True)
        first = jnp.min(jnp.where(g == m, idx, float(gate.shape[axis])), axis=axis, keepdims=True)
        pick = idx == first
        sel = jnp.where(jnp.logical_and(pick, m > -jnp.inf), 1.0, sel)
        g = jnp.where(pick, -jnp.inf, g)
    return sel


def _block_indicator(n_rows, n_cols, scale):
    blk = lax.broadcasted_iota(jnp.int32, (n_rows, n_cols), 1) // MOBA_BLOCK
    return jnp.where(blk == lax.broadcasted_iota(jnp.int32, (n_rows, n_cols), 0), scale, 0.0).astype(BF16)


def _moba_prompt_kernel(qb_ref, kvt_ref, o_ref, kext_ref, kmean_ref, qe_ref, s_ref, mx_ref, m_ref, acc_ref):
    own = pl.program_id(1)
    t = kvt_ref.shape[3]
    tq = qb_ref.shape[0]
    n_ext = kext_ref.shape[1] - HEAD_DIM

    @pl.when(own == 0)
    def _():
        ind = _block_indicator(n_ext, t, 1.0)
        for g in range(N_KV):
            kt = kvt_ref[0, 2 * g]
            kext_ref[g, :HEAD_DIM, :] = kt
            kext_ref[g, HEAD_DIM:, :] = ind
            kmean_ref[g] = lax.dot_general(ind, kt, _NT, preferred_element_type=F32) * (1.0 / MOBA_BLOCK)

    blk = lax.broadcasted_iota(jnp.int32, (n_ext, tq), 0)
    for h in range(N_HEADS):
        g = h // GROUP
        q = qb_ref[:, h * HEAD_DIM:(h + 1) * HEAD_DIM]
        gate_t = lax.dot_general(kmean_ref[g].astype(BF16), q, _NT, preferred_element_type=F32)
        picked_t = _top_blocks(gate_t, blk < own, 0)
        allowed = jnp.logical_or(picked_t > 0.5, blk == own)
        bias = _pad_rows(jnp.where(allowed, 0.0, MASKED), LANES).T[:, :n_ext]
        qe_ref[h] = jnp.concatenate([q * SM_SCALE, bias.astype(BF16)], axis=1)

    mx_ref[...] = jnp.full(mx_ref.shape, MASKED, F32)
    acc_ref[...] = jnp.zeros(acc_ref.shape, F32)

    def logits(n, extra_bias):
        k0 = pl.multiple_of(n * MOBA_BLOCK, MOBA_BLOCK)
        for g in range(N_KV):
            ke = kext_ref[g, :, pl.ds(k0, MOBA_BLOCK)]
            for r in range(GROUP):
                h = g * GROUP + r
                s = jnp.dot(qe_ref[h], ke, preferred_element_type=F32)
                _logits_step(n, h, s if extra_bias is None else s + extra_bias, s_ref, mx_ref)

    def logits_body(n, carry):
        logits(n, None)
        return carry

    lax.fori_loop(0, own, logits_body, 0)
    causal = (lax.broadcasted_iota(jnp.int32, (tq, MOBA_BLOCK), 1)
              <= lax.broadcasted_iota(jnp.int32, (tq, MOBA_BLOCK), 0))
    logits(own, jnp.where(causal, 0.0, MASKED))
    _row_max(mx_ref, m_ref)

    def values_body(n, carry):
        k0 = pl.multiple_of(n * MOBA_BLOCK, MOBA_BLOCK)
        for g in range(N_KV):
            vt = _with_ones(kvt_ref[0, 2 * g + 1, :, pl.ds(k0, MOBA_BLOCK)])
            for r in range(GROUP):
                _values_step(n, g * GROUP + r, vt, s_ref, m_ref, acc_ref)
        return carry

    lax.fori_loop(0, own + 1, values_body, 0)
    _softmax_finish(o_ref, acc_ref)


def _moba_prompt(pp, b, t):
    tq = MOBA_BLOCK
    n_t = t // tq
    assert t % MOBA_BLOCK == 0 and n_t <= LANES
    n_ext = -(-n_t // BF16_ROWS) * BF16_ROWS
    return pl.pallas_call(
        _moba_prompt_kernel, grid=(b, n_t),
        in_specs=[pl.BlockSpec((tq, WIDTH), lambda bb, i: (bb * n_t + i, 2)),
                  pl.BlockSpec((1, KV_SLABS, HEAD_DIM, t), lambda bb, i: (bb, 0, 0, 0))],
        out_specs=pl.BlockSpec((tq, WIDTH), lambda bb, i: (bb * n_t + i, 0)),
        out_shape=jax.ShapeDtypeStruct((b * t, WIDTH), BF16),
        scratch_shapes=[pltpu.VMEM((N_KV, HEAD_DIM + n_ext, t), BF16), pltpu.VMEM((N_KV, n_ext, HEAD_DIM), F32),
                        pltpu.VMEM((N_HEADS, tq, HEAD_DIM + n_ext), BF16),
                        pltpu.VMEM((n_t, N_HEADS, tq, MOBA_BLOCK), F32), pltpu.VMEM((N_HEADS, tq, LANES), F32),
                        pltpu.VMEM((N_HEADS, tq, 1), F32), pltpu.VMEM((N_HEADS, tq, ACC_W), F32)],
        compiler_params=_cparams("arbitrary", "arbitrary"), name="moba_prompt",
    )(pp["q16"], pp["kvbt16"])


SEQ_PER_STEP = 2
DECODE_CHUNK = 2048
SELECT_SEQ_PER_STEP = 8


def _page_copy(pt_ref, pool, buf, sems, j, seq, slot, u, p):
    off = pl.multiple_of(p * PAGE_SIZE, PAGE_SIZE)
    lead = (slice(None),) * (len(buf.shape) - 3)
    return pltpu.make_async_copy(pool.at[pt_ref[seq, p]], buf.at[(slot, u) + lead + (pl.ds(off, PAGE_SIZE),)],
                                 sems.at[2 * j + slot])


def _paged_fetch(pt_ref, pools, bufs, sems, past):
    b = pl.program_id(0)
    nb = pl.num_programs(0)
    slot = b % 2
    n_pages = past // PAGE_SIZE
    n_seq = bufs[0].shape[1]

    def each_copy(step, sl, p, fn):
        for u in range(n_seq):
            for j, (pool, buf) in enumerate(zip(pools, bufs)):
                fn(_page_copy(pt_ref, pool, buf, sems, j, step * n_seq + u, sl, u, p))

    def start(step, sl):
        def body(p, carry):
            each_copy(step, sl, p, lambda cp: cp.start())
            return carry
        lax.fori_loop(0, n_pages, body, 0)

    @pl.when(b == 0)
    def _():
        start(0, 0)

    @pl.when(b + 1 < nb)
    def _():
        start(b + 1, 1 - slot)

    def wait_body(p, carry):
        each_copy(b, slot, p, lambda cp: cp.wait())
        return carry
    lax.fori_loop(0, n_pages, wait_body, 0)
    return slot


def _stack_heads(q, heads, scale=1.0):
    qf = q.astype(F32) * scale
    return jnp.concatenate([qf[:, h * HEAD_DIM:(h + 1) * HEAD_DIM] for h in heads], axis=0).astype(BF16)


def _pad_rows(x, n):
    return jnp.concatenate([x, jnp.zeros((n - x.shape[0], x.shape[1]), x.dtype)], axis=0)


def _chunked_attention(queries, kt_of, vt_of, bias_of, bias_new, k_new, v_new, s_ref, chunk):
    n = len(queries)
    rows = queries[0].shape[0]
    n_chunks = s_ref.shape[2] // chunk

    def logits_body(c, mx):
        k0 = pl.multiple_of(c * chunk, chunk)
        out = []
        for i in range(n):
            s = jnp.dot(queries[i], kt_of(i, k0), preferred_element_type=F32) + bias_of(i, k0)
            s_ref[i, :, pl.ds(k0, chunk)] = s
            out.append(jnp.maximum(mx[i], _fold_lanes(s, jnp.maximum)))
        return tuple(out)

    mx = lax.fori_loop(0, n_chunks, logits_body, tuple(jnp.full((rows, LANES), MASKED, F32) for _ in range(n)))
    s_new = [lax.dot_general(queries[i], k_new[i], _NT, preferred_element_type=F32) + bias_new[i] for i in range(n)]
    m = [jnp.maximum(jnp.max(mx[i], axis=1, keepdims=True), jnp.max(s_new[i], axis=1, keepdims=True))
         for i in range(n)]

    def values_body(c, carry):
        k0 = pl.multiple_of(c * chunk, chunk)
        out = []
        for i in range(n):
            l, acc = carry[i]
            p = jnp.exp(s_ref[i, :, pl.ds(k0, chunk)] - m[i])
            out.append((l + _fold_lanes(p, jnp.add),
                        acc + lax.dot_general(p.astype(BF16), vt_of(i, k0), _NT, preferred_element_type=F32)))
        return tuple(out)

    init = tuple((jnp.zeros((rows, LANES), F32), jnp.zeros((rows, HEAD_DIM), F32)) for _ in range(n))
    carry = lax.fori_loop(0, n_chunks, values_body, init)
    outs = []
    for i in range(n):
        l, acc = carry[i]
        p_new = jnp.exp(s_new[i] - m[i])
        l = jnp.sum(l, axis=1, keepdims=True) + jnp.sum(p_new, axis=1, keepdims=True)
        acc = acc + jnp.dot(p_new.astype(BF16), v_new[i], preferred_element_type=F32)
        outs.append(acc / l)
    return outs


def _store_heads(o_ref, u, per_group, nq):
    outs = [per_group[g][r * nq:(r + 1) * nq, :] for g in range(N_KV) for r in range(GROUP)]
    o_ref[u] = jnp.concatenate(outs, axis=1).astype(o_ref.dtype)


def _new_kv(kv_new, g):
    k = kv_new[:, 2 * g * HEAD_DIM:(2 * g + 1) * HEAD_DIM]
    v = kv_new[:, (2 * g + 1) * HEAD_DIM:(2 * g + 2) * HEAD_DIM]
    return _pad_rows(k, LANES).astype(BF16), _pad_rows(v, LANES).astype(BF16)


def _units():
    return [(u, g) for u in range(SEQ_PER_STEP) for g in range(N_KV)]


def _dsa_select_kernel(pt_ref, q_ref, kiw_ref, idx_pool, bias_ref, idxbuf, score_ref, sems, *, past, k_sel, chunk):
    n_seq, nq = q_ref.shape[0], q_ref.shape[1]
    slot = _paged_fetch(pt_ref, (idx_pool,), (idxbuf,), sems, past)
    col_new = lax.broadcasted_iota(jnp.int32, (1, LANES), 1)
    qidx = lax.broadcasted_iota(jnp.int32, (nq, 1), 0)
    qi_rows = [_stack_heads(q_ref[u], range(IDX_HEADS)) for u in range(n_seq)]
    kiw = [kiw_ref[u] for u in range(n_seq)]

    def head_sum(s, u):
        score = jnp.zeros((nq, s.shape[1]), F32)
        for h in range(IDX_HEADS):
            score = score + jnp.maximum(s[h * nq:(h + 1) * nq, :], 0.0) * kiw[u][:, IDX_DIM + h:IDX_DIM + h + 1]
        return score

    def idx_body(c, carry):
        k0 = pl.multiple_of(c * chunk, chunk)
        for u in range(n_seq):
            keys = idxbuf[slot, u, :, pl.ds(k0, chunk)].astype(BF16)
            score_ref[0, u * nq:(u + 1) * nq, pl.ds(k0, chunk)] = head_sum(
                jnp.dot(qi_rows[u], keys, preferred_element_type=F32), u)
        return carry

    lax.fori_loop(0, past // chunk, idx_body, 0)
    for u in range(n_seq):
        k_new = _pad_rows(kiw[u][:, :IDX_DIM], LANES).astype(BF16)
        s = head_sum(lax.dot_general(qi_rows[u], k_new, _NT, preferred_element_type=F32), u)
        score_ref[0, u * nq:(u + 1) * nq, past:] = jnp.where(col_new <= qidx, s, -jnp.inf)

    n_valid = jnp.concatenate([past + qidx + 1] * n_seq, axis=0)
    thr, jthr = _select_threshold(score_ref, 1, n_valid, k_sel, 1)
    x = score_ref[0]
    col = lax.broadcasted_iota(jnp.int32, (1, x.shape[1]), 1).astype(F32)
    sel = jnp.logical_or(x > thr, jnp.logical_and(x == thr, col <= jthr))
    bias = jnp.where(sel, 0.0, MASKED)
    for u in range(n_seq):
        bias_ref[u] = bias[u * nq:(u + 1) * nq, :]


def _dsa_decode_kernel(pt_ref, q_ref, kva_ref, bias_ref, kv_pool, o_ref, kvbuf, s_ref, sems, *, past, chunk):
    nq = q_ref.shape[1]
    slot = _paged_fetch(pt_ref, (kv_pool,), (kvbuf,), sems, past)

    def bias_rows(u, k0, width):
        return jnp.concatenate([bias_ref[u, :, pl.ds(k0, width)]] * GROUP, axis=0)

    units = _units()
    queries = [_stack_heads(q_ref[u], range(g * GROUP, (g + 1) * GROUP), SM_SCALE) for u, g in units]
    new = [_new_kv(kva_ref[u], g) for u, g in units]
    outs = _chunked_attention(
        queries,
        lambda i, k0: kvbuf[slot, units[i][0], 2 * units[i][1], :, pl.ds(k0, chunk)].astype(BF16),
        lambda i, k0: kvbuf[slot, units[i][0], 2 * units[i][1] + 1, :, pl.ds(k0, chunk)].astype(BF16),
        lambda i, k0: bias_rows(units[i][0], k0, chunk),
        [bias_rows(u, past, LANES) for u, g in units], [kv[0] for kv in new], [kv[1] for kv in new],
        s_ref, chunk)
    for u in range(SEQ_PER_STEP):
        _store_heads(o_ref, u, outs[u * N_KV:(u + 1) * N_KV], nq)


def _moba_decode_kernel(pt_ref, q_ref, kvb_ref, kv_pool, o_ref, kvbuf, s_ref, sems, *, past, chunk):
    nq = q_ref.shape[1]
    n_blk = past // MOBA_BLOCK
    n_chunks = past // chunk
    slot = _paged_fetch(pt_ref, (kv_pool,), (kvbuf,), sems, past)
    rows = GROUP * nq
    units = _units()
    kt_of = lambda i, k0: kvbuf[slot, units[i][0], 2 * units[i][1], :, pl.ds(k0, chunk)].astype(BF16)

    def indicator(k0):
        blk = (k0 + lax.broadcasted_iota(jnp.int32, (n_blk, chunk), 1)) // MOBA_BLOCK
        return jnp.where(blk == lax.broadcasted_iota(jnp.int32, (n_blk, chunk), 0), 1.0, 0.0).astype(BF16)

    def mean_body(c, sums):
        k0 = pl.multiple_of(c * chunk, chunk)
        ind = indicator(k0)
        return tuple(sums[i] + lax.dot_general(ind, kt_of(i, k0), _NT, preferred_element_type=F32)
                     for i in range(len(units)))

    sums = lax.fori_loop(0, n_chunks, mean_body, tuple(jnp.zeros((n_blk, HEAD_DIM), F32) for _ in units))

    queries, bias_blk = [], []
    for i, (u, g) in enumerate(units):
        qb = q_ref[u]
        kmean = (sums[i] * (1.0 / MOBA_BLOCK)).astype(BF16)
        gate = lax.dot_general(_stack_heads(qb, range(g * GROUP, (g + 1) * GROUP)), kmean, _NT,
                               preferred_element_type=F32)
        picked = _top_blocks(gate, jnp.full(gate.shape, True), 1)
        bias_blk.append(jnp.where(picked > 0.5, 0.0, MASKED).astype(BF16))
        queries.append(_stack_heads(qb, range(g * GROUP, (g + 1) * GROUP), SM_SCALE))

    col = lax.broadcasted_iota(jnp.int32, (rows, LANES), 1)
    qidx = lax.broadcasted_iota(jnp.int32, (rows, LANES), 0) % nq
    bias_new = jnp.where(col <= qidx, 0.0, MASKED)
    new = [_new_kv(kvb_ref[u], g) for u, g in units]
    outs = _chunked_attention(
        queries, kt_of,
        lambda i, k0: kvbuf[slot, units[i][0], 2 * units[i][1] + 1, :, pl.ds(k0, chunk)].astype(BF16),
        lambda i, k0: jnp.dot(bias_blk[i], indicator(k0), preferred_element_type=F32),
        [bias_new] * len(units), [kv[0] for kv in new], [kv[1] for kv in new], s_ref, chunk)
    for u in range(SEQ_PER_STEP):
        _store_heads(o_ref, u, outs[u * N_KV:(u + 1) * N_KV], nq)


def _sample_specs(n_seq, nq, widths_blocks):
    return [pl.BlockSpec((n_seq, nq, w), lambda bb, pt, j=j: (bb, 0, j)) for w, j in widths_blocks]


def _dsa_decode(ps, page_table, idx_pool_t, kv_pool_t, db, nq):
    past = page_table.shape[1] * PAGE_SIZE
    lp = past + LANES
    k_sel = min(IDX_TOPK, (past + nq) // 4)
    chunk = math.gcd(past, DECODE_CHUNK)
    q16 = ps["q16"].reshape(db, nq, -1)
    n_sel = math.gcd(db, SELECT_SEQ_PER_STEP)
    assert db % SEQ_PER_STEP == 0

    bias = pl.pallas_call(
        functools.partial(_dsa_select_kernel, past=past, k_sel=k_sel, chunk=chunk),
        grid_spec=pltpu.PrefetchScalarGridSpec(
            num_scalar_prefetch=1, grid=(db // n_sel,),
            in_specs=_sample_specs(n_sel, nq, [(WIDTH, 1), (LANES, 0)]) + [pl.BlockSpec(memory_space=pl.ANY)],
            out_specs=_sample_specs(n_sel, nq, [(lp, 0)])[0],
            scratch_shapes=[pltpu.VMEM((2, n_sel, IDX_DIM, past), F32), pltpu.VMEM((1, n_sel * nq, lp), F32),
                            pltpu.SemaphoreType.DMA((2,))]),
        out_shape=jax.ShapeDtypeStruct((db, nq, lp), F32),
        compiler_params=_cparams("arbitrary"), name="dsa_select",
    )(page_table, q16, ps["kiw"].reshape(db, nq, LANES), idx_pool_t)

    return pl.pallas_call(
        functools.partial(_dsa_decode_kernel, past=past, chunk=chunk),
        grid_spec=pltpu.PrefetchScalarGridSpec(
            num_scalar_prefetch=1, grid=(db // SEQ_PER_STEP,),
            in_specs=_sample_specs(SEQ_PER_STEP, nq, [(WIDTH, 0), (KV_WIDTH, 0), (lp, 0)])
            + [pl.BlockSpec(memory_space=pl.ANY)],
            out_specs=_sample_specs(SEQ_PER_STEP, nq, [(WIDTH, 0)])[0],
            scratch_shapes=[pltpu.VMEM((2, SEQ_PER_STEP, KV_SLABS, HEAD_DIM, past), F32),
                            pltpu.VMEM((SEQ_PER_STEP * N_KV, GROUP * nq, past), F32),
                            pltpu.SemaphoreType.DMA((2,))]),
        out_shape=jax.ShapeDtypeStruct((db, nq, WIDTH), BF16),
        compiler_params=_cparams("arbitrary"), name="dsa_decode",
    )(page_table, q16, ps["kva"].reshape(db, nq, KV_WIDTH), bias, kv_pool_t)


def _moba_decode(ps, page_table, kv_pool_t, db, nq):
    past = page_table.shape[1] * PAGE_SIZE
    assert past % MOBA_BLOCK == 0 and nq <= MOBA_BLOCK and db % SEQ_PER_STEP == 0
    qw = ps["q16"].shape[1]
    kern = functools.partial(_moba_decode_kernel, past=past, chunk=math.gcd(past, DECODE_CHUNK))
    grid_spec = pltpu.PrefetchScalarGridSpec(
        num_scalar_prefetch=1, grid=(db // SEQ_PER_STEP,),
        in_specs=_sample_specs(SEQ_PER_STEP, nq, [(WIDTH, 2), (KV_WIDTH, 0)]) + [pl.BlockSpec(memory_space=pl.ANY)],
        out_specs=_sample_specs(SEQ_PER_STEP, nq, [(WIDTH, 0)])[0],
        scratch_shapes=[pltpu.VMEM((2, SEQ_PER_STEP, KV_SLABS, HEAD_DIM, past), F32),
                        pltpu.VMEM((SEQ_PER_STEP * N_KV, GROUP * nq, past), F32), pltpu.SemaphoreType.DMA((2,))])
    return pl.pallas_call(
        kern, grid_spec=grid_spec, out_shape=jax.ShapeDtypeStruct((db, nq, WIDTH), BF16),
        compiler_params=_cparams("arbitrary"), name="moba_decode",
    )(page_table, ps["q16"].reshape(db, nq, qw), ps["kvb"].reshape(db, nq, KV_WIDTH), kv_pool_t)


def _out_kernel(x_ref, oa_ref, ob_ref, gate_ref, wba_ref, wbb_ref, wo_ref, y_ref):
    d = x_ref.shape[1]
    ua = oa_ref[...] * gate_ref[:, 0:WIDTH]
    ub = ob_ref[...] * gate_ref[:, WIDTH:2 * WIDTH]
    a = jnp.dot(ua, wba_ref[...], preferred_element_type=F32)
    b = jnp.dot(ub, wbb_ref[...], preferred_element_type=F32)
    ga = gate_ref[:, 2 * WIDTH:2 * WIDTH + d].astype(F32)
    gb = gate_ref[:, 2 * WIDTH + d:2 * WIDTH + 2 * d].astype(F32)
    merged = (ga * a + gb * b).astype(BF16)
    y_ref[...] = x_ref[...] + jnp.dot(merged, wo_ref[...], preferred_element_type=F32)


def _out_proj(x2d, oa, ob, gate16, wts, tm):
    n, d = x2d.shape
    row = lambda w: pl.BlockSpec((tm, w), lambda i: (i, 0))
    return pl.pallas_call(
        _out_kernel, grid=(n // tm,),
        in_specs=[row(d), row(WIDTH), row(WIDTH), row(gate16.shape[1]),
                  _const_spec((WIDTH, d)), _const_spec((WIDTH, d)), _const_spec((d, d))],
        out_specs=row(d), out_shape=jax.ShapeDtypeStruct((n, d), F32),
        compiler_params=_cparams("arbitrary"), name="out_proj",
    )(x2d, oa, ob, gate16, wts["wba"], wts["wbb"], wts["wo"])


def _rope_angles(pos):
    inv = ROPE_THETA ** (-jnp.arange(ROT_HALF, dtype=F32) / ROT_HALF)
    ang = pos.astype(F32)[:, None] * inv[None, :]
    return jnp.cos(ang), jnp.sin(ang)


def _rope_tables(pos):
    p = pos.shape[0]
    c, s = _rope_angles(pos)
    rest = HEAD_DIM - 2 * ROT_HALF
    z8, zr = jnp.zeros((p, ROT_HALF), F32), jnp.zeros((p, rest), F32)
    c_head = jnp.concatenate([c, c, jnp.ones((p, rest), F32)], axis=1)
    s1_head = jnp.concatenate([-s, z8, zr], axis=1)
    s2_head = jnp.concatenate([z8, s, zr], axis=1)
    one, zero = jnp.ones((p, HEAD_DIM), F32), jnp.zeros((p, HEAD_DIM), F32)
    cat = lambda a, b: jnp.concatenate([a, b], axis=1)
    return (cat(c_head, c_head), cat(s1_head, s1_head), cat(s2_head, s2_head),
            cat(c_head, one), cat(s1_head, zero), cat(s2_head, zero))


def _prep_weights(norm_g, w_in, qn_a, kn_a, qn_i, kn_i, qn_b, kn_b, w_ba, w_bb, w_out):
    d = w_in.shape[0]
    kvw = N_KV * HEAD_DIM
    splits = (WIDTH, kvw, kvw, IDX_HEADS * IDX_DIM, IDX_DIM, IDX_HEADS, WIDTH, WIDTH, kvw, kvw, WIDTH, d, d)
    offs = np.concatenate([[0], np.cumsum(splits)])
    qa, ka, va, qi, ki, wi, za, qb, kb, vb, zb, ga, gb = [w_in[:, offs[j]:offs[j + 1]] for j in range(13)]
    hd = HEAD_DIM
    pad = jnp.zeros((d, LANES - IDX_DIM - IDX_HEADS), F32)
    slabs = jnp.concatenate([ka[:, :hd], va[:, :hd], ka[:, hd:], va[:, hd:],
                             kb[:, :hd], vb[:, :hd], kb[:, hd:], vb[:, hd:], ki], axis=1)
    wk = jnp.concatenate([slabs, wi, pad], axis=1)
    wwi = jnp.concatenate([wi, jnp.zeros((d, LANES - IDX_HEADS), F32)], axis=1)
    zero = jnp.zeros((hd,), F32)
    one = jnp.ones((hd,), F32)
    wi_gain = jnp.concatenate([jnp.full((IDX_HEADS,), (IDX_HEADS * IDX_DIM) ** -0.5, F32),
                               jnp.zeros((hd - IDX_HEADS,), F32)])
    gka = jnp.concatenate([kn_a, zero, kn_a, zero, kn_b, zero, kn_b, zero, kn_i, zero])[None, :]
    gkb = jnp.concatenate([zero, one, zero, one, zero, one, zero, one, zero, wi_gain])[None, :]
    gkt = jnp.concatenate([kn_a, one, kn_a, one, kn_b, one, kn_b, one, kn_i])[:, None]
    gq = jnp.concatenate([jnp.tile(qn_a, N_HEADS), jnp.tile(qn_i, IDX_HEADS), jnp.tile(qn_b, N_HEADS)])[None, :]
    blk = np.arange(MXU_DIM) // HEAD_DIM
    bd = jnp.asarray((blk[:, None] == blk[None, :]) / HEAD_DIM, BF16)
    return dict(
        ng=norm_g[None, :].astype(F32),
        wq=jnp.concatenate([qa, qi, qb], axis=1).astype(BF16), wk=wk.astype(BF16),
        wkt=slabs.T.astype(BF16), wwi=wwi.astype(BF16),
        wg=jnp.concatenate([za, zb, ga, gb], axis=1).astype(BF16),
        gq=gq, gka=gka, gkb=gkb, gkt=gkt, bd=bd,
        wba=w_ba.astype(BF16), wbb=w_bb.astype(BF16), wo=w_out.astype(BF16))


def kernel(x_prompt, x_sample, cache_kv_a, cache_idx_k, cache_kv_b, page_table, norm_g, w_in, q_norm_a, k_norm_a,
           idx_q_norm, idx_k_norm, q_norm_b, k_norm_b, w_branch_a, w_branch_b, w_out):
    b, t, d = x_prompt.shape
    db, nq, _ = x_sample.shape
    depth = w_in.shape[0]
    n_phys = cache_kv_a.shape[1]
    past = page_table.shape[1] * PAGE_SIZE
    tm = 512
    tm_s = min(tm, db * nq)
    assert t % tm == 0 and tm_s % nq == 0 and (db * nq) % tm_s == 0

    pos_p = jnp.arange(t, dtype=jnp.int32)
    cos_p, sin_p = _rope_angles(pos_p)
    tab_p = _rope_tables(pos_p)[:3] + (cos_p.T, sin_p.T)
    tab_s = tuple(jnp.tile(a, (tm_s // nq, 1)) for a in _rope_tables(past + jnp.arange(nq, dtype=jnp.int32)))

    kv_t = lambda pool: jnp.transpose(pool, (0, 2, 3, 4, 1)).reshape(n_phys, KV_SLABS, HEAD_DIM, PAGE_SIZE)

    hp = x_prompt.reshape(b * t, d)
    hs = x_sample.reshape(db * nq, d)
    new = [[] for _ in range(6)]
    for l in range(depth):
        wts = _prep_weights(norm_g[l], w_in[l], q_norm_a[l], k_norm_a[l], idx_q_norm[l], idx_k_norm[l],
                            q_norm_b[l], k_norm_b[l], w_branch_a[l], w_branch_b[l], w_out[l])
        pp = _project_prompt(hp, tab_p, wts, b, t, tm)
        oa = _dsa_prompt(pp, b, t, tq=256, kc=min(512, t))
        ob = _moba_prompt(pp, b, t)
        hp = _out_proj(hp, oa, ob, pp["gate16"], wts, math.gcd(b * t, 2 * tm))

        ps = _project_sample(hs, tab_s, wts, tm_s)
        oa_s = _dsa_decode(ps, page_table, jnp.transpose(cache_idx_k[l], (0, 2, 1)), kv_t(cache_kv_a[l]), db, nq)
        ob_s = _moba_decode(ps, page_table, kv_t(cache_kv_b[l]), db, nq)
        hs = _out_proj(hs, oa_s.reshape(db * nq, WIDTH), ob_s.reshape(db * nq, WIDTH), ps["gate16"], wts, tm_s)

        to_tokens = lambda a: jnp.transpose(a.reshape(b, N_KV, 2, HEAD_DIM, t), (0, 4, 1, 2, 3))
        new[0].append(to_tokens(pp["kvat"]))
        new[1].append(jnp.transpose(pp["kit"], (0, 2, 1)))
        new[2].append(to_tokens(pp["kvbt"]))
        new[3].append(ps["kva"].reshape(db, nq, N_KV, 2, HEAD_DIM))
        new[4].append(ps["kiw"][:, :IDX_DIM].reshape(db, nq, IDX_DIM))
        new[5].append(ps["kvb"].reshape(db, nq, N_KV, 2, HEAD_DIM))
    return (hp.reshape(b, t, d), hs.reshape(db, nq, d)) + tuple(jnp.stack(a, axis=0) for a in new)
```
